```python
import math
import jax, jax.numpy as jnp
from jax import lax
import numpy as np

D_MODEL = 1024
BATCH = 16
SEQ = 4096
DEPTH = 2
DEC_BATCH = 16
DEC_SEQ = 32
PAST_LEN = 4096

CHUNK = 64
D_LRU = 1024
LRU_BLOCKS = 16
LRU_BW = D_LRU // LRU_BLOCKS
LRU_C = 8.0
CONV_W = 4
M_HEADS = 4
M_HD = 256
D_M = M_HEADS * M_HD
CONV_CH = D_LRU + 2 * D_M
IN_COLS = CONV_CH + 2 * D_M + 2 * M_HEADS + 2 * D_MODEL
N_GROUPS = 4
EXP_PER_GROUP = 4
N_EXPERTS = N_GROUPS * EXP_PER_GROUP
TOP_K_IN_GROUP = 2
D_EXPERT = 512
EPS = 1e-6

kernel_name = "hawk_mlstm_hmoe_streaming_step"


def _rmsnorm(x, g):
    x32 = x.astype(jnp.float32)
    y = x32 * lax.rsqrt(jnp.mean(x32 * x32, axis=-1, keepdims=True) + EPS)
    return (y * g.astype(jnp.float32)).astype(x.dtype)


def _causal_conv(u, prev, w, b):
    L = u.shape[1]
    full = jnp.concatenate([prev.astype(u.dtype), u], axis=1)
    out = full[:, 0:L] * w[0]
    for tap in range(1, CONV_W):
        out = out + full[:, tap:tap + L] * w[tap]
    return out + b, full[:, -(CONV_W - 1):]


def _rg_lru(x, h0, w_r, b_r, w_i, b_i, lam):
    Bn, L, _ = x.shape
    f32 = jnp.float32
    xb = x.reshape(Bn, L, LRU_BLOCKS, LRU_BW)
    r = jax.nn.sigmoid(jnp.einsum('blnc,ncd->blnd', xb, w_r).reshape(Bn, L, D_LRU) + b_r)
    i = jax.nn.sigmoid(jnp.einsum('blnc,ncd->blnd', xb, w_i).reshape(Bn, L, D_LRU) + b_i)
    log_a = -LRU_C * r.astype(f32) * jax.nn.softplus(-lam.astype(f32))
    a = jnp.exp(log_a)
    u = jnp.sqrt(-jnp.expm1(2.0 * log_a)) * (i * x).astype(f32)

    def comb(c1, c2):
        a1, b1 = c1
        a2, b2 = c2
        return a1 * a2, a2 * b1 + b2

    a_cum, b_cum = lax.associative_scan(comb, (a, u), axis=1)
    h = b_cum + a_cum * h0.astype(f32)[:, None]
    return h.astype(x.dtype), h[:, -1]


def _block_len(L):
    if L <= CHUNK:
        return L
    return CHUNK if L % CHUNK == 0 else math.gcd(L, CHUNK)


def _mlstm(q, k, v, i_pre, f_pre, C0, n0, m0):
    Bn, L, H, Dh = q.shape
    T = _block_len(L)
    NC = L // T
    f32 = jnp.float32

    def vec_chunks(a):
        return a.astype(f32).reshape(Bn, NC, T, H, Dh).transpose(1, 0, 3, 2, 4)

    def gate_chunks(a):
        return a.astype(f32).reshape(Bn, NC, T, H).transpose(1, 0, 3, 2)

    qc = vec_chunks(q)
    kc = vec_chunks(k) * (Dh ** -0.5)
    vc = vec_chunks(v)
    ic = gate_chunks(i_pre)
    lfc = gate_chunks(jax.nn.log_sigmoid(f_pre.astype(f32)))
    tri = jnp.tril(jnp.ones((T, T), dtype=bool))

    def step(carry, inp):
        C, n, m = carry
        qb, kb, vb, ib, lf = inp
        b = jnp.cumsum(lf, axis=-1)
        dlog = b[..., :, None] - b[..., None, :] + ib[..., None, :]
        dlog = jnp.where(tri, dlog, -jnp.inf)
        m_inter = b + m[..., None]
        m_t = jnp.maximum(m_inter, jnp.max(dlog, axis=-1))
        s = jnp.einsum('bhtd,bhsd->bhts', qb, kb) * jnp.exp(dlog - m_t[..., None])
        inter = jnp.exp(m_inter - m_t)
        num = jnp.einsum('bhts,bhsd->bhtd', s, vb) + inter[..., None] * jnp.einsum('bhvk,bhtk->bhtv', C, qb)
        den = jnp.sum(s, axis=-1) + inter * jnp.einsum('bhk,bhtk->bht', n, qb)
        h = num / jnp.maximum(jnp.abs(den), jnp.exp(-m_t))[..., None]
        bT = b[..., -1]
        m_new = m_t[..., -1]
        wk = jnp.exp(bT[..., None] - b + ib - m_new[..., None])
        decay = jnp.exp(bT + m - m_new)
        C_new = decay[..., None, None] * C + jnp.einsum('bhtv,bhtk->bhvk', vb * wk[..., None], kb)
        n_new = decay[..., None] * n + jnp.einsum('bht,bhtk->bhk', wk, kb)
        return (C_new, n_new, m_new), h

    (C, n, m), hs = lax.scan(step, (C0.astype(f32), n0.astype(f32), m0.astype(f32)),
                             (qc, kc, vc, ic, lfc))
    h = hs.transpose(1, 0, 3, 2, 4).reshape(Bn, L, H, Dh)
    return h, C, n, m


def _hmoe(x, w_group, b_group, w_erouter, b_erouter, w_gate, w_up, w_down):
    Bn, L, D = x.shape
    t = x.reshape(-1, D)
    f32 = jnp.float32
    gl = (t @ w_group + b_group).astype(f32)
    gp = jax.nn.softmax(gl, axis=-1)
    _, g_idx = lax.top_k(gl, 1)
    p_g = jnp.take_along_axis(gp, g_idx, axis=-1)
    el = (jnp.einsum('nd,gde->nge', t, w_erouter) + b_erouter).astype(f32)
    el_sel = jnp.take_along_axis(el, g_idx[:, :, None], axis=1)[:, 0]
    top_v, top_i = lax.top_k(el_sel, TOP_K_IN_GROUP)
    w2 = jax.nn.softmax(top_v, axis=-1) * p_g
    expert_id = g_idx * EXP_PER_GROUP + top_i
    combine = jnp.sum(jax.nn.one_hot(expert_id, N_EXPERTS, dtype=f32) * w2[..., None], axis=1)
    combine = combine.astype(x.dtype)
    out = jnp.zeros_like(t)
    for e in range(N_EXPERTS):
        hid = jax.nn.silu(t @ w_gate[e]) * (t @ w_up[e])
        out = out + combine[:, e:e + 1] * (hid @ w_down[e])
    return out.reshape(Bn, L, D)


def _layer(x, conv_prev, h0, C0, n0, m0, p):
    (g_mix, w_in, b_igate, b_fgate, conv_w, conv_b, lru_wr, lru_br, lru_wi, lru_bi, lru_lambda,
     m_gn, w_pa, w_pb, w_out, g_ffn, w_group, b_group, w_erouter, b_erouter,
     w_gate, w_up, w_down) = p
    Bn, L, _ = x.shape
    xn = _rmsnorm(x, g_mix)
    z = xn @ w_in
    c0 = CONV_CH
    v = z[..., c0:c0 + D_M]
    o = z[..., c0 + D_M:c0 + 2 * D_M]
    off = c0 + 2 * D_M
    i_pre = z[..., off:off + M_HEADS] + b_igate
    f_pre = z[..., off + M_HEADS:off + 2 * M_HEADS] + b_fgate
    off2 = off + 2 * M_HEADS
    g_a = z[..., off2:off2 + D_MODEL]
    g_b = z[..., off2 + D_MODEL:]
    conv_out, conv_state = _causal_conv(z[..., :c0], conv_prev, conv_w, conv_b)
    lru_y, h_last = _rg_lru(conv_out[..., :D_LRU], h0, lru_wr, lru_br, lru_wi, lru_bi, lru_lambda)
    qk = jax.nn.silu(conv_out[..., D_LRU:])
    q = qk[..., :D_M].reshape(Bn, L, M_HEADS, M_HD)
    k = qk[..., D_M:].reshape(Bn, L, M_HEADS, M_HD)
    hm, C, n, m = _mlstm(q, k, v.reshape(Bn, L, M_HEADS, M_HD), i_pre, f_pre, C0, n0, m0)
    hm = hm * lax.rsqrt(jnp.mean(hm * hm, axis=-1, keepdims=True) + EPS)
    m_y = (hm.reshape(Bn, L, D_M) * m_gn.astype(jnp.float32)).astype(x.dtype) * jax.nn.sigmoid(o)
    merged = jax.nn.sigmoid(g_a) * (lru_y @ w_pa) + jax.nn.sigmoid(g_b) * (m_y @ w_pb)
    x = x + merged @ w_out
    x = x + _hmoe(_rmsnorm(x, g_ffn), w_group, b_group, w_erouter, b_erouter, w_gate, w_up, w_down)
    dt = x.dtype
    return x, (conv_state.astype(dt), h_last.astype(dt), C.astype(dt), n.astype(dt), m.astype(dt))


def _trunk(x, conv0, h00, C00, n00, m00, params, g_final):
    convs, hs, Cs, ns, ms = [], [], [], [], []
    for l in range(DEPTH):
        p = [a[l] for a in params]
        x, (cs, hl, C, n, m) = _layer(x, conv0[l], h00[l], C00[l], n00[l], m00[l], p)
        convs.append(cs); hs.append(hl); Cs.append(C); ns.append(n); ms.append(m)
    y = _rmsnorm(x, g_final)
    return y, jnp.stack(convs), jnp.stack(hs), jnp.stack(Cs), jnp.stack(ns), jnp.stack(ms)


def setup_inputs(seed: int = 0) -> dict:
    key = jax.random.key(seed)
    ks = jax.random.split(key, 40)
    f32 = jnp.float32

    def nrm(k, shape, scale):
        return jax.random.normal(k, shape, f32) * scale

    u = jax.random.uniform(ks[15], (DEPTH, D_LRU), f32, 0.9, 0.999)
    s = u ** (1.0 / LRU_C)
    lru_lambda = jnp.log(s) - jnp.log1p(-s)
    return {
        "x_prompt": nrm(ks[0], (BATCH, SEQ, D_MODEL), 1.0),
        "x_sample": nrm(ks[1], (DEC_BATCH, DEC_SEQ, D_MODEL), 1.0),
        "state_conv": nrm(ks[2], (DEPTH, DEC_BATCH, CONV_W - 1, CONV_CH), 1.0),
        "state_lru": nrm(ks[3], (DEPTH, DEC_BATCH, D_LRU), 0.5),
        "state_mlstm_C": nrm(ks[4], (DEPTH, DEC_BATCH, M_HEADS, M_HD, M_HD), 0.05),
        "state_mlstm_n": jnp.abs(nrm(ks[5], (DEPTH, DEC_BATCH, M_HEADS, M_HD), 0.1)),
        "state_mlstm_m": nrm(ks[6], (DEPTH, DEC_BATCH, M_HEADS), 0.5),
        "g_mix": 1.0 + nrm(ks[7], (DEPTH, D_MODEL), 0.02),
        "w_in": nrm(ks[8], (DEPTH, D_MODEL, IN_COLS), D_MODEL ** -0.5),
        "b_igate": nrm(ks[9], (DEPTH, M_HEADS), 0.1),
        "b_fgate": jnp.linspace(3.0, 6.0, M_HEADS, dtype=f32)[None] + nrm(ks[10], (DEPTH, M_HEADS), 0.1),
        "conv_w": nrm(ks[11], (DEPTH, CONV_W, CONV_CH), CONV_W ** -0.5),
        "conv_b": nrm(ks[12], (DEPTH, CONV_CH), 0.01),
        "lru_wr": nrm(ks[13], (DEPTH, LRU_BLOCKS, LRU_BW, LRU_BW), LRU_BW ** -0.5),
        "lru_br": nrm(ks[14], (DEPTH, D_LRU), 0.01),
        "lru_wi": nrm(ks[16], (DEPTH, LRU_BLOCKS, LRU_BW, LRU_BW), LRU_BW ** -0.5),
        "lru_bi": nrm(ks[17], (DEPTH, D_LRU), 0.01),
        "lru_lambda": lru_lambda,
        "m_gn": 1.0 + nrm(ks[18], (DEPTH, D_M), 0.02),
        "w_pa": nrm(ks[19], (DEPTH, D_LRU, D_MODEL), D_LRU ** -0.5),
        "w_pb": nrm(ks[20], (DEPTH, D_M, D_MODEL), D_M ** -0.5),
        "w_out": nrm(ks[21], (DEPTH, D_MODEL, D_MODEL), D_MODEL ** -0.5),
        "g_ffn": 1.0 + nrm(ks[22], (DEPTH, D_MODEL), 0.02),
        "w_group": nrm(ks[23], (DEPTH, D_MODEL, N_GROUPS), D_MODEL ** -0.5),
        "b_group": nrm(ks[24], (DEPTH, N_GROUPS), 0.01),
        "w_erouter": nrm(ks[25], (DEPTH, N_GROUPS, D_MODEL, EXP_PER_GROUP), D_MODEL ** -0.5),
        "b_erouter": nrm(ks[26], (DEPTH, N_GROUPS, EXP_PER_GROUP), 0.01),
        "w_gate": nrm(ks[27], (DEPTH, N_EXPERTS, D_MODEL, D_EXPERT), D_MODEL ** -0.5),
        "w_up": nrm(ks[28], (DEPTH, N_EXPERTS, D_MODEL, D_EXPERT), D_MODEL ** -0.5),
        "w_down": nrm(ks[29], (DEPTH, N_EXPERTS, D_EXPERT, D_MODEL), D_EXPERT ** -0.5),
        "g_final": 1.0 + nrm(ks[30], (D_MODEL,), 0.02),
    }


def reference(x_prompt, x_sample, state_conv, state_lru, state_mlstm_C, state_mlstm_n, state_mlstm_m,
              g_mix, w_in, b_igate, b_fgate, conv_w, conv_b, lru_wr, lru_br, lru_wi, lru_bi, lru_lambda,
              m_gn, w_pa, w_pb, w_out, g_ffn, w_group, b_group, w_erouter, b_erouter,
              w_gate, w_up, w_down, g_final):
    params = (g_mix, w_in, b_igate, b_fgate, conv_w, conv_b, lru_wr, lru_br, lru_wi, lru_bi, lru_lambda,
              m_gn, w_pa, w_pb, w_out, g_ffn, w_group, b_group, w_erouter, b_erouter,
              w_gate, w_up, w_down)
    dt = x_prompt.dtype
    Bp = x_prompt.shape[0]
    zc = jnp.zeros((DEPTH, Bp, CONV_W - 1, CONV_CH), dt)
    zh = jnp.zeros((DEPTH, Bp, D_LRU), dt)
    zC = jnp.zeros((DEPTH, Bp, M_HEADS, M_HD, M_HD), dt)
    zn = jnp.zeros((DEPTH, Bp, M_HEADS, M_HD), dt)
    zm = jnp.zeros((DEPTH, Bp, M_HEADS), dt)
    y_prompt, p_conv, p_lru, p_C, p_n, p_m = _trunk(x_prompt, zc, zh, zC, zn, zm, params, g_final)
    y_sample, s_conv, s_lru, s_C, s_n, s_m = _trunk(x_sample, state_conv, state_lru, state_mlstm_C,
                                                    state_mlstm_n, state_mlstm_m, params, g_final)
    return (y_prompt, y_sample, p_conv, p_lru, p_C, p_n, p_m, s_conv, s_lru, s_C, s_n, s_m)
```

```python
import functools

import jax
import jax.numpy as jnp
from jax import lax
from jax.experimental import pallas as pl
from jax.experimental.pallas import tpu as pltpu

F32 = jnp.float32
BF16 = jnp.bfloat16

EPS = 1e-6
LRU_C = 8.0
CONV_W = 4
N_GROUPS = 4
EXP_PER_GROUP = 4
TOP_K = 2

V7X_VMEM_BYTES = 64 * 2**20
V7X_LANES = 128
V7X_SUBLANES = 8
V7X_MXU_DIM = 256

VMEM_LIMIT_BYTES = V7X_VMEM_BYTES - 8 * 2**20

ROUTE_ROWS = 32
GATE_LANES = V7X_LANES


def _rmsnorm(x, g):
    return (x * lax.rsqrt(jnp.mean(x * x, axis=-1, keepdims=True) + EPS)) * g


def _sigmoid(x):
    return jax.nn.sigmoid(x)


def _log_sigmoid(x):
    return jnp.minimum(x, 0.0) - jnp.log1p(jnp.exp(-jnp.abs(x)))


def _softplus(x):
    return jnp.maximum(x, 0.0) + jnp.log1p(jnp.exp(-jnp.abs(x)))


def _dot(a, b):
    return jnp.dot(a, b, preferred_element_type=F32)


def _dot_nt(a, b, precision=None):
    return lax.dot_general(a, b, (((1,), (1,)), ((), ())), precision=precision,
                           preferred_element_type=F32)


def _dot_tn(a, b):
    return lax.dot_general(a, b, (((0,), (0,)), ((), ())), preferred_element_type=F32)


def _inproj_kernel(*refs, n_parts):
    part_refs = refs[:n_parts]
    g_ref, w_ref = refs[n_parts:n_parts + 2]
    out_refs = refs[n_parts + 2:]
    if n_parts > 1:
        xres_ref, z_ref, xn_s = out_refs
    else:
        z_ref, xn_s = out_refs

    @pl.when(pl.program_id(1) == 0)
    def _():
        x = part_refs[0][...]
        for r in part_refs[1:]:
            x = x + r[...]
        if n_parts > 1:
            xres_ref[...] = x
        xn_s[...] = _rmsnorm(x, g_ref[...]).astype(BF16)

    z_ref[...] = _dot(xn_s[...], w_ref[...])


def _inproj(x, moe_out, g, w_all, *, tm, tn):
    n, d = x.shape
    zc = w_all.shape[1]
    nrt = n // tm
    parts = [x]
    part_specs = [pl.BlockSpec((tm, d), lambda i, j: (i, 0))]
    if moe_out is not None:
        parts += [moe_out, moe_out]
        part_specs += [pl.BlockSpec((tm, d), lambda i, j: (i, 0)),
                       pl.BlockSpec((tm, d), lambda i, j: (i + nrt, 0))]
    z_shape = jax.ShapeDtypeStruct((n, zc), F32)
    z_spec = pl.BlockSpec((tm, tn), lambda i, j: (i, j))
    if moe_out is not None:
        out_shape = (jax.ShapeDtypeStruct((n, d), F32), z_shape)
        out_specs = (pl.BlockSpec((tm, d), lambda i, j: (i, 0)), z_spec)
    else:
        out_shape = z_shape
        out_specs = z_spec
    res = pl.pallas_call(
        functools.partial(_inproj_kernel, n_parts=len(parts)),
        grid=(nrt, zc // tn),
        in_specs=part_specs + [pl.BlockSpec((1, d), lambda i, j: (0, 0)),
                               pl.BlockSpec((d, tn), lambda i, j: (0, j))],
        out_specs=out_specs,
        out_shape=out_shape,
        scratch_shapes=[pltpu.VMEM((tm, d), BF16)],
        compiler_params=pltpu.CompilerParams(
            dimension_semantics=("parallel", "arbitrary"),
            vmem_limit_bytes=VMEM_LIMIT_BYTES),
        name="inproj",
    )(*parts, g, w_all)
    if moe_out is not None:
        return res
    return x, res


def _mixer_kernel(zc_ref, zv_ref, zo_ref, zga_ref, zgb_ref, zg_ref, x_ref,
                  conv0_ref, h0_ref, c0_ref, n0_ref, m0_ref,
                  cw_ref, cb_ref, wri_ref, br_ref, bi_ref, lam_ref, gbias_ref, mgn_ref,
                  wpa_ref, wpb_ref, wout_ref, gffn_ref, wrt_ref, brt_ref,
                  xnew_ref, xn2_ref, rid_ref, rw_ref,
                  convs_ref, hlast_ref, cout_ref, nout_ref, mout_ref,
                  zbuf, xl_s, q_s, k_s, a_s, u_s, h_s, hm_s, c_s, n_s, m_s, hc_s,
                  *, chunk, heads):
    t = x_ref.shape[0]
    d = x_ref.shape[1]
    hd = d // heads
    step = pl.program_id(1)
    last_step = pl.num_programs(1) - 1
    tail = V7X_SUBLANES
    first_tap = tail - (CONV_W - 1)

    @pl.when(step == 0)
    def _init():
        zbuf[0:tail, :] = conv0_ref[0]
        hc_s[...] = h0_ref[0]
        c_s[...] = c0_ref[0]
        n_s[...] = n0_ref[0]
        m_s[...] = m0_ref[0]

    zbuf[tail:tail + t, :] = zc_ref[...]

    def conv_cols(lo, hi):
        acc = zbuf[first_tap:first_tap + t, lo:hi] * cw_ref[0:1, lo:hi]
        for tap in range(1, CONV_W):
            acc = acc + zbuf[first_tap + tap:first_tap + tap + t, lo:hi] * cw_ref[tap:tap + 1, lo:hi]
        return acc + cb_ref[:, lo:hi]

    xl_s[...] = conv_cols(0, d)
    qc = conv_cols(d, 2 * d)
    q_s[...] = (qc * _sigmoid(qc)).astype(BF16)
    kc = conv_cols(2 * d, 3 * d)
    k_s[...] = ((kc * _sigmoid(kc)) * (hd ** -0.5)).astype(BF16)

    @pl.when(step == last_step)
    def _conv_state():
        convs_ref[0] = zbuf[t:t + tail, :]
    zbuf[0:tail, :] = zbuf[t:t + tail, :]

    gw = V7X_MXU_DIM
    sp = _softplus(-lam_ref[...])
    for g in range(d // gw):
        lo, hi = g * gw, (g + 1) * gw
        xg = xl_s[:, lo:hi]
        ri = _dot(xg.astype(BF16), wri_ref[g])
        r = _sigmoid(ri[:, :gw] + br_ref[:, lo:hi])
        ig = _sigmoid(ri[:, gw:] + bi_ref[:, lo:hi])
        log_a = (-LRU_C * r) * sp[:, lo:hi]
        a = jnp.exp(log_a)
        mult = jnp.sqrt(-jnp.tanh(log_a) * (a * a + 1.0))
        a_s[:, lo:hi] = a
        u_s[:, lo:hi] = mult * (ig * xg)

    rows = lax.broadcasted_iota(jnp.int32, (V7X_SUBLANES, d), 0)
    shifts = (1, 2, 4)
    masks = [rows >= s for s in shifts]

    def scan_group(gi, hc):
        r0 = pl.multiple_of(gi * V7X_SUBLANES, V7X_SUBLANES)
        a = a_s[pl.ds(r0, V7X_SUBLANES), :]
        u = u_s[pl.ds(r0, V7X_SUBLANES), :]
        for s, mk in zip(shifts, masks):
            a_sh = jnp.where(mk, pltpu.roll(a, s, 0), 1.0)
            u_sh = jnp.where(mk, pltpu.roll(u, s, 0), 0.0)
            u = a * u_sh + u
            a = a * a_sh
        h = u + a * hc
        h_s[pl.ds(r0, V7X_SUBLANES), :] = h
        return h[V7X_SUBLANES - 1:V7X_SUBLANES, :]

    hc = lax.fori_loop(0, t // V7X_SUBLANES, scan_group, hc_s[...], unroll=2)
    hc_s[...] = hc

    @pl.when(step == last_step)
    def _lru_state():
        hlast_ref[0] = hc

    tc = chunk
    gates = zg_ref[...] + gbias_ref[...]
    lf = _log_sigmoid(gates)
    tri_r = lax.broadcasted_iota(jnp.int32, (tc, tc), 0)
    tri_c = lax.broadcasted_iota(jnp.int32, (tc, tc), 1)
    causal = tri_c <= tri_r
    tri = causal.astype(F32)
    lane = lax.broadcasted_iota(jnp.int32, (tc, GATE_LANES), 1)
    for j in range(t // tc):
        r0, r1 = j * tc, (j + 1) * tc
        bcum = jnp.dot(tri, lf[r0:r1], precision=lax.Precision.HIGHEST,
                       preferred_element_type=F32)
        b_al = pltpu.roll(bcum, GATE_LANES - heads, 1)
        g_al = gates[r0:r1] - b_al
        for h in range(heads):
            c0, c1 = h * hd, (h + 1) * hd
            qh = q_s[r0:r1, c0:c1]
            kh = k_s[r0:r1, c0:c1]
            vh = zv_ref[r0:r1, c0:c1]
            bcol = b_al[:, h:h + 1]
            gcol = g_al[:, h:h + 1]
            sel = (lane == h).astype(F32)
            grow = _dot_nt(sel, g_al, precision=lax.Precision.HIGHEST)
            m_prev = m_s[h:h + 1, 0:1]
            dlog = jnp.where(causal, bcol + grow, -jnp.inf)
            m_inter = bcol + m_prev
            m_t = jnp.maximum(m_inter, jnp.max(dlog, axis=-1, keepdims=True))
            s = _dot_nt(qh, kh) * jnp.exp(dlog - m_t)
            inter = jnp.exp(m_inter - m_t)
            qc_ = _dot_nt(qh, c_s[h].astype(BF16))
            num = _dot(s.astype(BF16), vh.astype(BF16)) + inter * qc_
            qn = jnp.sum(qh.astype(F32) * n_s[h:h + 1, :], axis=-1, keepdims=True)
            den = jnp.sum(s, axis=-1, keepdims=True) + inter * qn
            hm_s[r0:r1, c0:c1] = num / jnp.maximum(jnp.abs(den), jnp.exp(-m_t))
            b_end = bcol[tc - 1:tc, :]
            m_new = m_t[tc - 1:tc, :]
            wk = jnp.exp(b_end + gcol - m_new)
            decay = jnp.exp(b_end + m_prev - m_new)
            c_s[h] = decay * c_s[h] + _dot_tn((vh * wk).astype(BF16), kh)
            n_s[h:h + 1, :] = decay * n_s[h:h + 1, :] + jnp.sum(
                wk * kh.astype(F32), axis=0, keepdims=True)
            m_s[h:h + 1, :] = jnp.broadcast_to(m_new, (1, V7X_LANES))

    @pl.when(step == last_step)
    def _mlstm_state():
        cout_ref[0] = c_s[...]
        nout_ref[0] = n_s[...]
        mout_ref[0] = m_s[...]

    for h in range(heads):
        c0, c1 = h * hd, (h + 1) * hd
        seg = hm_s[:, c0:c1]
        seg = seg * lax.rsqrt(jnp.mean(seg * seg, axis=-1, keepdims=True) + EPS)
        hm_s[:, c0:c1] = (seg * mgn_ref[:, c0:c1]) * _sigmoid(zo_ref[:, c0:c1])
    pa = _dot(h_s[...].astype(BF16), wpa_ref[...])
    pb = _dot(hm_s[...].astype(BF16), wpb_ref[...])
    merged = _sigmoid(zga_ref[...]) * pa + _sigmoid(zgb_ref[...]) * pb
    xnew = x_ref[...] + _dot(merged.astype(BF16), wout_ref[...])
    xnew_ref[...] = xnew
    xn2 = _rmsnorm(xnew, gffn_ref[...])
    xn2_ref[...] = xn2

    logit = _dot_nt(wrt_ref[...], xn2, precision=lax.Precision.HIGHEST) + brt_ref[...]
    ng, ne = N_GROUPS, EXP_PER_GROUP
    gl = logit[0:ng]
    gidx = lax.broadcasted_iota(jnp.int32, (ng, t), 0).astype(F32)
    idx = lax.broadcasted_iota(jnp.int32, (ne, t), 0).astype(F32)
    gmax = jnp.max(gl, axis=0, keepdims=True)
    g_idx = jnp.min(jnp.where(gl == gmax, gidx, float(ng)), axis=0, keepdims=True)
    p_g = 1.0 / jnp.sum(jnp.exp(gl - gmax), axis=0, keepdims=True)
    el = jnp.zeros((ne, t), F32)
    for g in range(ng):
        el = jnp.where(g_idx == float(g), logit[ng + g * ne:ng + (g + 1) * ne], el)
    v1 = jnp.max(el, axis=0, keepdims=True)
    i1 = jnp.min(jnp.where(el == v1, idx, float(ne)), axis=0, keepdims=True)
    rest = jnp.where(idx == i1, -jnp.inf, el)
    v2 = jnp.max(rest, axis=0, keepdims=True)
    i2 = jnp.min(jnp.where(rest == v2, idx, float(ne)), axis=0, keepdims=True)
    e21 = jnp.exp(v2 - v1)
    w1 = (1.0 / (1.0 + e21)) * p_g
    w2 = (e21 / (1.0 + e21)) * p_g
    e1 = g_idx * float(ne) + i1
    e2 = g_idx * float(ne) + i2
    slot = lax.broadcasted_iota(jnp.int32, (V7X_SUBLANES, t), 0)
    rid_ref[0] = jnp.where(slot == 0, e1, jnp.where(slot == 1, e2, 0.0)).astype(jnp.int32)
    rw_ref[0] = jnp.where(slot == 0, w1, jnp.where(slot == 1, w2, 0.0))


def _mixer(z, x, conv0p, h0, c0, n0, m0p, p, *, batch, seq, t, chunk):
    n, d = x.shape
    heads = n0.shape[1]
    hd = n0.shape[2]
    nt = seq // t
    cc = conv0p.shape[2]
    zblk = lambda width, col: pl.BlockSpec((t, width), lambda b, c: (b * nt + c, col))
    const = lambda shape: pl.BlockSpec(shape, lambda b, c: (0,) * len(shape))
    per_b = lambda shape: pl.BlockSpec((1,) + shape, lambda b, c: (b,) + (0,) * len(shape))
    tok = pl.BlockSpec((t, d), lambda b, c: (b * nt + c, 0))
    gate_col = (cc + 4 * d) // GATE_LANES
    in_specs = [
        zblk(cc, 0), zblk(d, cc // d), zblk(d, cc // d + 1), zblk(d, cc // d + 2),
        zblk(d, cc // d + 3), zblk(GATE_LANES, gate_col), tok,
        per_b((V7X_SUBLANES, cc)), per_b((1, d)), per_b((heads, hd, hd)), per_b((heads, hd)),
        per_b((V7X_SUBLANES, V7X_LANES)),
        const((CONV_W, cc)), const((1, cc)), const(p["wri"].shape), const((1, d)), const((1, d)),
        const((1, d)), const((1, GATE_LANES)), const((1, d)),
        const((d, d)), const((d, d)), const((d, d)), const((1, d)),
        const((ROUTE_ROWS, d)), const((ROUTE_ROWS, 1)),
    ]
    route_blk = pl.BlockSpec((1, V7X_SUBLANES, t), lambda b, c: (b * nt + c, 0, 0))
    out_specs = (
        tok, tok, route_blk, route_blk,
        per_b((V7X_SUBLANES, cc)), per_b((1, d)), per_b((heads, hd, hd)), per_b((heads, hd)),
        per_b((V7X_SUBLANES, V7X_LANES)),
    )
    out_shape = (
        jax.ShapeDtypeStruct((n, d), F32), jax.ShapeDtypeStruct((n, d), F32),
        jax.ShapeDtypeStruct((batch * nt, V7X_SUBLANES, t), jnp.int32),
        jax.ShapeDtypeStruct((batch * nt, V7X_SUBLANES, t), F32),
        jax.ShapeDtypeStruct((batch, V7X_SUBLANES, cc), F32),
        jax.ShapeDtypeStruct((batch, 1, d), F32),
        jax.ShapeDtypeStruct((batch, heads, hd, hd), F32),
        jax.ShapeDtypeStruct((batch, heads, hd), F32),
        jax.ShapeDtypeStruct((batch, V7X_SUBLANES, V7X_LANES), F32),
    )
    scratch = [
        pltpu.VMEM((t + V7X_SUBLANES, cc), F32),
        pltpu.VMEM((t, d), F32),
        pltpu.VMEM((t, d), BF16),
        pltpu.VMEM((t, d), BF16),
        pltpu.VMEM((t, d), F32),
        pltpu.VMEM((t, d), F32),
        pltpu.VMEM((t, d), F32),
        pltpu.VMEM((t, d), F32),
        pltpu.VMEM((heads, hd, hd), F32),
        pltpu.VMEM((heads, hd), F32),
        pltpu.VMEM((V7X_SUBLANES, V7X_LANES), F32),
        pltpu.VMEM((1, d), F32),
    ]
    return pl.pallas_call(
        functools.partial(_mixer_kernel, chunk=chunk, heads=heads),
        grid=(batch, nt),
        in_specs=in_specs,
        out_specs=out_specs,
        out_shape=out_shape,
        scratch_shapes=scratch,
        compiler_params=pltpu.CompilerParams(
            dimension_semantics=("parallel", "arbitrary"),
            vmem_limit_bytes=VMEM_LIMIT_BYTES),
        name="mixer",
    )(z, z, z, z, z, z, x, conv0p, h0, c0, n0, m0p,
      p["conv_w"], p["conv_b"], p["wri"], p["b_r"], p["b_i"], p["lam"], p["gate_bias"], p["m_gn"],
      p["w_pa"], p["w_pb"], p["w_out"], p["g_ffn"], p["wrt"], p["brt"])


def _expert_kernel(sg_ref, st_ref, slo_ref, shi_ref, sfirst_ref, slast_ref,
                   a_ref, wcol_ref, xn_hbm, wgu_ref, wd_ref, o_hbm,
                   xbuf, xb16, ybuf, sem_in, sem_out, *, n_tokens):
    del sg_ref, st_ref
    s = pl.program_id(0)
    tm = xbuf.shape[0]
    de = wd_ref.shape[1]

    def row_in(r, tok):
        return pltpu.make_async_copy(xn_hbm.at[pl.ds(tok, 1)], xbuf.at[pl.ds(r, 1)], sem_in)

    def row_out(r, dst):
        return pltpu.make_async_copy(ybuf.at[pl.ds(r, 1)], o_hbm.at[pl.ds(dst, 1)], sem_out)

    @pl.when(sfirst_ref[s] == 1)
    def _gather():
        def start(r, carry):
            a = a_ref[0, 0, r]
            row_in(r, jnp.where(a >= n_tokens, a - n_tokens, a)).start()
            return carry

        def wait(r, carry):
            row_in(r, 0).wait()
            return carry
        lax.fori_loop(0, tm, start, 0, unroll=8)
        lax.fori_loop(0, tm, wait, 0, unroll=8)
        xb16[...] = xbuf[...].astype(BF16)

    lo = slo_ref[s]
    hi = shi_ref[s]

    @pl.when(hi > lo)
    def _compute():
        gu = _dot(xb16[...], wgu_ref[0])
        gate = gu[:, :de]
        hid = (gate * _sigmoid(gate)) * gu[:, de:]
        y = _dot(hid.astype(BF16), wd_ref[0]) * wcol_ref[...]
        rows = lax.broadcasted_iota(jnp.int32, (tm, 1), 0)
        mine = (rows >= lo) & (rows < hi)

        @pl.when(sfirst_ref[s] == 1)
        def _():
            ybuf[...] = y

        @pl.when(sfirst_ref[s] == 0)
        def _():
            ybuf[...] = jnp.where(mine, y, ybuf[...])

    @pl.when(slast_ref[s] == 1)
    def _scatter():
        def start(r, carry):
            row_out(r, a_ref[0, 0, r]).start()
            return carry

        def wait(r, carry):
            row_out(r, 0).wait()
            return carry
        lax.fori_loop(0, tm, start, 0, unroll=8)
        lax.fori_loop(0, tm, wait, 0, unroll=8)


def _experts(xn2, e_flat, w_flat, wgu, wd, *, tm):
    n, d = xn2.shape
    n_exp, de, _ = wd.shape
    na = e_flat.shape[0]
    ntile = na // tm
    nstep = ntile + n_exp - 1

    e_sorted, a_sorted, w_sorted = lax.sort(
        (e_flat, lax.iota(jnp.int32, na), w_flat), num_keys=1, is_stable=True)
    cnt = jnp.sum((e_flat[:, None] == jnp.arange(n_exp, dtype=jnp.int32)[None, :]).astype(jnp.int32), axis=0)
    ends = jnp.cumsum(cnt)
    starts = ends - cnt
    first_tile = starts // tm
    last_tile = jnp.maximum(ends - 1, 0) // tm
    tiles_per = jnp.where(cnt > 0, last_tile - first_tile + 1, 0)
    step_ends = jnp.cumsum(tiles_per)
    total = step_ends[-1]
    sidx = jnp.arange(nstep, dtype=jnp.int32)
    valid = sidx < total
    sc = jnp.minimum(sidx, total - 1)
    grp = jnp.sum((sc[:, None] >= step_ends[None, :]).astype(jnp.int32), axis=1)
    tile = first_tile[grp] + (sc - (step_ends[grp] - tiles_per[grp]))
    lo = jnp.maximum(starts[grp], tile * tm) - tile * tm
    hi = jnp.minimum(ends[grp], (tile + 1) * tm) - tile * tm
    lo = jnp.where(valid, lo, 0)
    hi = jnp.where(valid, hi, 0)
    prev_tile = jnp.concatenate([jnp.full((1,), -1, jnp.int32), tile[:-1]])
    next_tile = jnp.concatenate([tile[1:], jnp.full((1,), -1, jnp.int32)])
    first = (valid & (tile != prev_tile)).astype(jnp.int32)
    last = (valid & ((tile != next_tile) | (sidx == total - 1))).astype(jnp.int32)

    grid_spec = pltpu.PrefetchScalarGridSpec(
        num_scalar_prefetch=6,
        grid=(nstep,),
        in_specs=[
            pl.BlockSpec((1, 1, tm), lambda s, sg, st, *_: (st[s], 0, 0), memory_space=pltpu.SMEM),
            pl.BlockSpec((tm, 1), lambda s, sg, st, *_: (st[s], 0)),
            pl.BlockSpec(memory_space=pl.ANY),
            pl.BlockSpec((1, d, 2 * de), lambda s, sg, st, *_: (sg[s], 0, 0)),
            pl.BlockSpec((1, de, d), lambda s, sg, st, *_: (sg[s], 0, 0)),
        ],
        out_specs=pl.BlockSpec(memory_space=pl.ANY),
        scratch_shapes=[
            pltpu.VMEM((tm, d), F32), pltpu.VMEM((tm, d), BF16), pltpu.VMEM((tm, d), F32),
            pltpu.SemaphoreType.DMA, pltpu.SemaphoreType.DMA,
        ],
    )
    return pl.pallas_call(
        functools.partial(_expert_kernel, n_tokens=n),
        grid_spec=grid_spec,
        out_shape=jax.ShapeDtypeStruct((na, d), F32),
        compiler_params=pltpu.CompilerParams(
            dimension_semantics=("arbitrary",),
            vmem_limit_bytes=VMEM_LIMIT_BYTES),
        name="experts",
    )(grp.astype(jnp.int32), tile.astype(jnp.int32), lo.astype(jnp.int32), hi.astype(jnp.int32),
      first, last, a_sorted.reshape(ntile, 1, tm), w_sorted.reshape(na, 1), xn2, wgu, wd)


def _final_kernel(x_ref, o0_ref, o1_ref, g_ref, y_ref):
    y_ref[...] = _rmsnorm(x_ref[...] + o0_ref[...] + o1_ref[...], g_ref[...])


def _final(x, moe_out, g, *, tm):
    n, d = x.shape
    nrt = n // tm
    return pl.pallas_call(
        _final_kernel,
        grid=(nrt,),
        in_specs=[pl.BlockSpec((tm, d), lambda i: (i, 0)),
                  pl.BlockSpec((tm, d), lambda i: (i, 0)),
                  pl.BlockSpec((tm, d), lambda i: (i + nrt, 0)),
                  pl.BlockSpec((1, d), lambda i: (0, 0))],
        out_specs=pl.BlockSpec((tm, d), lambda i: (i, 0)),
        out_shape=jax.ShapeDtypeStruct((n, d), F32),
        compiler_params=pltpu.CompilerParams(dimension_semantics=("parallel",)),
        name="final_norm",
    )(x, moe_out, moe_out, g)


def _block_diag_tiles(w):
    nb, bw, _ = w.shape
    per = V7X_MXU_DIM // bw
    w4 = w.reshape(nb // per, per, bw, bw)
    eye = jnp.eye(per, dtype=w.dtype)
    return jnp.einsum("gicd,ij->gicjd", w4, eye).reshape(nb // per, per * bw, per * bw)


def _layer_params(l, g_mix, w_in, b_igate, b_fgate, conv_w, conv_b, lru_wr, lru_br, lru_wi, lru_bi,
                  lru_lambda, m_gn, w_pa, w_pb, w_out, g_ffn, w_group, b_group, w_erouter, b_erouter,
                  w_gate, w_up, w_down):
    d = w_in.shape[1]
    cc = conv_w.shape[2]
    heads = b_igate.shape[1]
    dm = m_gn.shape[1]
    w = w_in[l]
    gate0 = cc + 2 * dm
    w_gates = jnp.pad(w[:, gate0:gate0 + 2 * heads], ((0, 0), (0, GATE_LANES - 2 * heads)))
    w_all = jnp.concatenate([w[:, :gate0], w[:, gate0 + 2 * heads:], w_gates], axis=1).astype(BF16)
    gate_bias = jnp.pad(jnp.concatenate([b_igate[l], b_fgate[l]]), (0, GATE_LANES - 2 * heads))[None]
    wri = jnp.concatenate([_block_diag_tiles(lru_wr[l]), _block_diag_tiles(lru_wi[l])], axis=2).astype(BF16)
    ng, _, ne = w_erouter.shape[1:]
    wr = jnp.concatenate([w_group[l], jnp.transpose(w_erouter[l], (1, 0, 2)).reshape(d, ng * ne)], axis=1)
    br = jnp.concatenate([b_group[l], b_erouter[l].reshape(ng * ne)])
    pad = ROUTE_ROWS - wr.shape[1]
    return dict(
        g_mix=g_mix[l][None], w_all=w_all, gate_bias=gate_bias,
        conv_w=conv_w[l], conv_b=conv_b[l][None], wri=wri,
        b_r=lru_br[l][None], b_i=lru_bi[l][None], lam=lru_lambda[l][None], m_gn=m_gn[l][None],
        w_pa=w_pa[l].astype(BF16), w_pb=w_pb[l].astype(BF16), w_out=w_out[l].astype(BF16),
        g_ffn=g_ffn[l][None],
        wrt=jnp.pad(wr, ((0, 0), (0, pad))).T, brt=jnp.pad(br, (0, pad))[:, None],
        wgu=jnp.concatenate([w_gate[l], w_up[l]], axis=2).astype(BF16), wd=w_down[l].astype(BF16),
    )


def _tile_sizes(n, seq):
    t = min(256, seq)
    chunk = min(128, t)
    tm = min(512, n)
    tm_exp = min(512, 2 * n)
    return t, chunk, tm, tm_exp


def _trunk(x, conv0, h00, c00, n00, m00, layers, g_final):
    batch, seq, d = x.shape
    n = batch * seq
    depth = len(layers)
    t, chunk, tm, tm_exp = _tile_sizes(n, seq)
    xf = x.reshape(n, d)
    moe_out = None
    convs, hs, cs, ns, ms = [], [], [], [], []
    heads = n00.shape[2]
    for l in range(depth):
        p = layers[l]
        tn = p["w_all"].shape[1] // 3
        xf, z = _inproj(xf, moe_out, p["g_mix"], p["w_all"], tm=tm, tn=tn)
        conv0p = jnp.pad(conv0[l], ((0, 0), (V7X_SUBLANES - (CONV_W - 1), 0), (0, 0)))
        m0p = jnp.broadcast_to(
            jnp.pad(m00[l], ((0, 0), (0, V7X_SUBLANES - heads)))[:, :, None],
            (batch, V7X_SUBLANES, V7X_LANES))
        (xf, xn2, rid, rw, conv_s, h_last, c_new, n_new, m_new) = _mixer(
            z, xf, conv0p, h00[l][:, None, :], c00[l], n00[l], m0p, p,
            batch=batch, seq=seq, t=t, chunk=chunk)
        e_flat = jnp.concatenate([rid[:, 0, :].reshape(n), rid[:, 1, :].reshape(n)])
        w_flat = jnp.concatenate([rw[:, 0, :].reshape(n), rw[:, 1, :].reshape(n)])
        moe_out = _experts(xn2, e_flat, w_flat, p["wgu"], p["wd"], tm=tm_exp)
        convs.append(conv_s[:, V7X_SUBLANES - (CONV_W - 1):, :])
        hs.append(h_last[:, 0, :])
        cs.append(c_new)
        ns.append(n_new)
        ms.append(m_new[:, :heads, 0])
    y = _final(xf, moe_out, g_final[None], tm=tm).reshape(batch, seq, d)
    return y, jnp.stack(convs), jnp.stack(hs), jnp.stack(cs), jnp.stack(ns), jnp.stack(ms)


def kernel(x_prompt, x_sample, state_conv, state_lru, state_mlstm_C, state_mlstm_n, state_mlstm_m,
           g_mix, w_in, b_igate, b_fgate, conv_w, conv_b, lru_wr, lru_br, lru_wi, lru_bi, lru_lambda,
           m_gn, w_pa, w_pb, w_out, g_ffn, w_group, b_group, w_erouter, b_erouter,
           w_gate, w_up, w_down, g_final):
    depth = w_in.shape[0]
    layers = [
        _layer_params(l, g_mix, w_in, b_igate, b_fgate, conv_w, conv_b, lru_wr, lru_br, lru_wi, lru_bi,
                      lru_lambda, m_gn, w_pa, w_pb, w_out, g_ffn, w_group, b_group, w_erouter,
                      b_erouter, w_gate, w_up, w_down)
        for l in range(depth)]
    dt = x_prompt.dtype
    bp = x_prompt.shape[0]
    zeros_like_state = lambda s: jnp.zeros((depth, bp) + s.shape[2:], dt)
    y_p, p_conv, p_lru, p_c, p_n, p_m = _trunk(
        x_prompt, zeros_like_state(state_conv), zeros_like_state(state_lru),
        zeros_like_state(state_mlstm_C), zeros_like_state(state_mlstm_n),
        zeros_like_state(state_mlstm_m), layers, g_final)
    y_s, s_conv, s_lru, s_c, s_n, s_m = _trunk(
        x_sample, state_conv, state_lru, state_mlstm_C, state_mlstm_n, state_mlstm_m,
        layers, g_final)
    return (y_p, y_s, p_conv, p_lru, p_c, p_n, p_m, s_conv, s_lru, s_c, s_n, s_m)
```

```python
import functools

import jax
import jax.numpy as jnp
from jax import lax
from jax.experimental import pallas as pl
from jax.experimental.pallas import tpu as pltpu

F32 = jnp.float32
BF16 = jnp.bfloat16

EPS = 1e-6
LRU_C = 8.0
CONV_W = 4
N_GROUPS = 4
EXP_PER_GROUP = 4

V7X_VMEM_BYTES = 64 * 2**20
V7X_LANES = 128
V7X_SUBLANES = 8
V7X_MXU_DIM = 256

VMEM_LIMIT_BYTES = V7X_VMEM_BYTES - 8 * 2**20

ROUTE_ROWS = 32
GATE_LANES = V7X_LANES


def _rmsnorm(x, g):
    return (x * lax.rsqrt(jnp.mean(x * x, axis=-1, keepdims=True) + EPS)) * g


def _sigmoid(x):
    return 0.5 * jnp.tanh(0.5 * x) + 0.5


def _silu(x):
    h = 0.5 * x
    return h * jnp.tanh(h) + h


def _log_sigmoid(x):
    return jnp.minimum(x, 0.0) - jnp.log1p(jnp.exp(-jnp.abs(x)))


def _softplus(x):
    return jnp.maximum(x, 0.0) + jnp.log1p(jnp.exp(-jnp.abs(x)))


def _dot(a, b):
    return jnp.dot(a, b, preferred_element_type=F32)


def _dot_nt(a, b, precision=None):
    return lax.dot_general(a, b, (((1,), (1,)), ((), ())), precision=precision,
                           preferred_element_type=F32)


def _split_bf16(x):
    hi = x.astype(BF16)
    return hi, (x - hi.astype(F32)).astype(BF16)


def _rows_from_tiles(ref, n_rows, d):
    return jnp.concatenate(
        [ref[pl.ds(j, n_rows, stride=V7X_SUBLANES), :] for j in range(d // V7X_LANES)], axis=1)


def _rows_to_tiles(ref, x):
    n_rows, d = x.shape
    for j in range(d // V7X_LANES):
        ref[pl.ds(j, n_rows, stride=V7X_SUBLANES), :] = x[:, j * V7X_LANES:(j + 1) * V7X_LANES]


def _inproj_kernel(*refs, with_moe, t):
    if with_moe:
        x_ref, o0_ref, o1_ref, g_ref, w_ref, wv_ref, wg_ref, xres_ref, z_ref, vt_ref, zg_ref, xn_s = refs
    else:
        x_ref, g_ref, w_ref, wv_ref, wg_ref, z_ref, vt_ref, zg_ref, xn_s = refs
    tm, d = x_ref.shape
    j = pl.program_id(1)

    @pl.when(j == 0)
    def _():
        x = x_ref[...]
        if with_moe:
            x = x + _rows_from_tiles(o0_ref, tm, d) + _rows_from_tiles(o1_ref, tm, d)
            xres_ref[...] = x
        xn_s[...] = _rmsnorm(x, g_ref[...]).astype(BF16)
        zg_ref[...] = _dot(xn_s[...], wg_ref[...])
        v = _dot(xn_s[...], wv_ref[...])
        tp = -(-t // V7X_LANES) * V7X_LANES
        for q in range(tm // t):
            vq = v[q * t:(q + 1) * t, :]
            if tp != t:
                vq = jnp.concatenate([vq, jnp.zeros((tp - t, d), F32)], axis=0)
            vt_ref[q] = vq.T[:, :t]

    @pl.when(j > 0)
    def _():
        z_ref[...] = _dot(xn_s[...], w_ref[...])


def _inproj(x, moe_out, p, *, tm, tn, t):
    n, d = x.shape
    zc = p["w_z"].shape[1]
    nrt = n // tm
    rt = V7X_SUBLANES
    parts = [x]
    part_specs = [pl.BlockSpec((tm, d), lambda i, j: (i, 0))]
    if moe_out is not None:
        parts += [moe_out, moe_out]
        part_specs += [pl.BlockSpec((tm * rt, V7X_LANES), lambda i, j: (i, 0)),
                       pl.BlockSpec((tm * rt, V7X_LANES), lambda i, j: (i + nrt, 0))]
    out_shape = [jax.ShapeDtypeStruct((n, zc), F32),
                 jax.ShapeDtypeStruct((n // t, d, t), F32),
                 jax.ShapeDtypeStruct((n, GATE_LANES), F32)]
    zcol = lambda j: jnp.maximum(j - 1, 0)
    out_specs = [pl.BlockSpec((tm, tn), lambda i, j: (i, zcol(j))),
                 pl.BlockSpec((tm // t, d, t), lambda i, j: (i, 0, 0)),
                 pl.BlockSpec((tm, GATE_LANES), lambda i, j: (i, 0))]
    if moe_out is not None:
        out_shape = [jax.ShapeDtypeStruct((n, d), F32)] + out_shape
        out_specs = [pl.BlockSpec((tm, d), lambda i, j: (i, 0))] + out_specs
    res = pl.pallas_call(
        functools.partial(_inproj_kernel, with_moe=moe_out is not None, t=t),
        grid=(nrt, zc // tn + 1),
        in_specs=part_specs + [pl.BlockSpec((1, d), lambda i, j: (0, 0)),
                               pl.BlockSpec((d, tn), lambda i, j: (0, zcol(j))),
                               pl.BlockSpec((d, d), lambda i, j: (0, 0)),
                               pl.BlockSpec((d, GATE_LANES), lambda i, j: (0, 0))],
        out_specs=out_specs,
        out_shape=out_shape,
        scratch_shapes=[pltpu.VMEM((tm, d), BF16)],
        compiler_params=pltpu.CompilerParams(
            dimension_semantics=("parallel", "arbitrary"),
            vmem_limit_bytes=VMEM_LIMIT_BYTES),
        name="inproj",
    )(*parts, p["g_mix"], p["w_z"], p["w_v"], p["w_gates"])
    if moe_out is not None:
        return res
    return [x] + list(res)


def _mixer_kernel(zc_ref, zo_ref, zga_ref, zgb_ref, vt_ref, zg_ref, x_ref,
                  conv0_ref, h0_ref, c0_ref, n0_ref, m0_ref,
                  cw_ref, cb_ref, wri_ref, br_ref, bi_ref, lam_ref, gbias_ref, mgn_ref,
                  wpa_ref, wpb_ref, wout_ref, gffn_ref, wrh_ref, wrl_ref, brt_ref,
                  xnew_ref, xn2_ref, rid_ref, rw_ref,
                  convs_ref, hlast_ref, cout_ref, nout_ref, mout_ref,
                  zbuf, xl_s, q_s, k_s, a_s, u_s, h_s, hm_s, c_s, n_s, m_s, hc_s,
                  *, chunk, heads):
    t = x_ref.shape[0]
    d = x_ref.shape[1]
    hd = d // heads
    step = pl.program_id(1)
    last_step = pl.num_programs(1) - 1
    tail = V7X_SUBLANES
    first_tap = tail - (CONV_W - 1)

    @pl.when(step == 0)
    def _init():
        zbuf[0:tail, :] = conv0_ref[0]
        hc_s[...] = h0_ref[0]
        c_s[...] = c0_ref[0]
        n_s[...] = n0_ref[0]
        m_s[...] = m0_ref[0]

    zbuf[tail:tail + t, :] = zc_ref[...]

    def conv_cols(lo, hi):
        acc = zbuf[first_tap:first_tap + t, lo:hi] * cw_ref[0:1, lo:hi]
        for tap in range(1, CONV_W):
            acc = acc + zbuf[first_tap + tap:first_tap + tap + t, lo:hi] * cw_ref[tap:tap + 1, lo:hi]
        return acc + cb_ref[:, lo:hi]

    q_s[...] = _silu(conv_cols(d, 2 * d)).astype(BF16)
    k_s[...] = (_silu(conv_cols(2 * d, 3 * d)) * (hd ** -0.5)).astype(BF16)

    tc = chunk
    gates = zg_ref[...] + gbias_ref[...]
    lf = _log_sigmoid(gates)
    tri_r = lax.broadcasted_iota(jnp.int32, (tc, tc), 0)
    tri_c = lax.broadcasted_iota(jnp.int32, (tc, tc), 1)
    causal = tri_c <= tri_r
    tri = causal.astype(F32)
    pick = (lax.broadcasted_iota(jnp.int32, (V7X_SUBLANES, GATE_LANES), 0) ==
            lax.broadcasted_iota(jnp.int32, (V7X_SUBLANES, GATE_LANES), 1)).astype(F32)
    for j in range(t // tc):
        r0, r1 = j * tc, (j + 1) * tc
        bcum = jnp.dot(tri, lf[r0:r1], precision=lax.Precision.HIGHEST,
                       preferred_element_type=F32)
        b_al = pltpu.roll(bcum, GATE_LANES - heads, 1)
        g_al = gates[r0:r1] - b_al
        g_rows = _dot_nt(pick, g_al, precision=lax.Precision.HIGHEST)
        for h in range(heads):
            c0, c1 = h * hd, (h + 1) * hd
            qh = q_s[r0:r1, c0:c1]
            kh = k_s[r0:r1, c0:c1]
            vth = vt_ref[0, c0:c1, r0:r1].astype(BF16)
            bcol = b_al[:, h:h + 1]
            gcol = g_al[:, h:h + 1]
            m_prev = m_s[h:h + 1, 0:1]
            dlog = jnp.where(causal, bcol + g_rows[h:h + 1, :], -jnp.inf)
            m_inter = bcol + m_prev
            m_t = jnp.maximum(m_inter, jnp.max(dlog, axis=-1, keepdims=True))
            s = _dot_nt(qh, kh) * jnp.exp(dlog - m_t)
            inter = jnp.exp(m_inter - m_t)
            qc_ = _dot_nt(qh, c_s[h].astype(BF16))
            num = _dot_nt(s.astype(BF16), vth) + inter * qc_
            qn = jnp.sum(qh.astype(F32) * n_s[h:h + 1, :], axis=-1, keepdims=True)
            den = jnp.sum(s, axis=-1, keepdims=True) + inter * qn
            hm_s[r0:r1, c0:c1] = num / jnp.maximum(jnp.abs(den), jnp.exp(-m_t))
            b_end = bcol[tc - 1:tc, :]
            m_new = m_t[tc - 1:tc, :]
            wk = jnp.exp(b_end + gcol - m_new)
            decay = jnp.exp(b_end + m_prev - m_new)
            kw = wk * kh.astype(F32)
            c_s[h] = decay * c_s[h] + _dot(vth, kw.astype(BF16))
            n_s[h:h + 1, :] = decay * n_s[h:h + 1, :] + jnp.sum(kw, axis=0, keepdims=True)
            m_s[h:h + 1, :] = jnp.broadcast_to(m_new, (1, V7X_LANES))

    xl_s[...] = conv_cols(0, d)
    zbuf[0:tail, :] = zbuf[t:t + tail, :]
    gw = V7X_MXU_DIM
    sp = _softplus(-lam_ref[...])
    for g in range(d // gw):
        lo, hi = g * gw, (g + 1) * gw
        xg = xl_s[:, lo:hi]
        ri = _dot(xg.astype(BF16), wri_ref[g])
        r = _sigmoid(ri[:, :gw] + br_ref[:, lo:hi])
        ig = _sigmoid(ri[:, gw:] + bi_ref[:, lo:hi])
        log_a = (-LRU_C * r) * sp[:, lo:hi]
        a = jnp.exp(log_a)
        mult = jnp.sqrt(-jnp.tanh(log_a) * (a * a + 1.0))
        a_s[:, lo:hi] = a
        u_s[:, lo:hi] = mult * (ig * xg)

    rows = lax.broadcasted_iota(jnp.int32, (V7X_SUBLANES, d), 0)
    shifts = (1, 2, 4)
    masks = [rows >= s for s in shifts]
    hc = hc_s[...]
    for gi in range(t // V7X_SUBLANES):
        r0 = gi * V7X_SUBLANES
        a = a_s[r0:r0 + V7X_SUBLANES, :]
        u = u_s[r0:r0 + V7X_SUBLANES, :]
        for s, mk in zip(shifts, masks):
            a_sh = jnp.where(mk, pltpu.roll(a, s, 0), 1.0)
            u_sh = jnp.where(mk, pltpu.roll(u, s, 0), 0.0)
            u = a * u_sh + u
            a = a * a_sh
        hg = u + a * hc
        h_s[r0:r0 + V7X_SUBLANES, :] = hg
        hc = hg[V7X_SUBLANES - 1:V7X_SUBLANES, :]
    hc_s[...] = hc

    for h in range(heads):
        c0, c1 = h * hd, (h + 1) * hd
        seg = hm_s[:, c0:c1]
        seg = seg * lax.rsqrt(jnp.mean(seg * seg, axis=-1, keepdims=True) + EPS)
        hm_s[:, c0:c1] = (seg * mgn_ref[:, c0:c1]) * _sigmoid(zo_ref[:, c0:c1])
    pa = _dot(h_s[...].astype(BF16), wpa_ref[...])
    pb = _dot(hm_s[...].astype(BF16), wpb_ref[...])
    merged = _sigmoid(zga_ref[...]) * pa + _sigmoid(zgb_ref[...]) * pb
    xnew = x_ref[...] + _dot(merged.astype(BF16), wout_ref[...])
    xnew_ref[...] = xnew
    xn2 = _rmsnorm(xnew, gffn_ref[...])
    _rows_to_tiles(xn2_ref, xn2)

    xh, xl = _split_bf16(xn2)
    logit = (_dot_nt(wrh_ref[...], xh) + _dot_nt(wrh_ref[...], xl) + _dot_nt(wrl_ref[...], xh)
             + brt_ref[...])
    ng, ne = N_GROUPS, EXP_PER_GROUP
    gl = logit[0:ng]
    gidx = lax.broadcasted_iota(jnp.int32, (ng, t), 0).astype(F32)
    idx = lax.broadcasted_iota(jnp.int32, (ne, t), 0).astype(F32)
    gmax = jnp.max(gl, axis=0, keepdims=True)
    g_idx = jnp.min(jnp.where(gl == gmax, gidx, float(ng)), axis=0, keepdims=True)
    p_g = 1.0 / jnp.sum(jnp.exp(gl - gmax), axis=0, keepdims=True)
    el = jnp.zeros((ne, t), F32)
    for g in range(ng):
        el = jnp.where(g_idx == float(g), logit[ng + g * ne:ng + (g + 1) * ne], el)
    v1 = jnp.max(el, axis=0, keepdims=True)
    i1 = jnp.min(jnp.where(el == v1, idx, float(ne)), axis=0, keepdims=True)
    rest = jnp.where(idx == i1, -jnp.inf, el)
    v2 = jnp.max(rest, axis=0, keepdims=True)
    i2 = jnp.min(jnp.where(rest == v2, idx, float(ne)), axis=0, keepdims=True)
    e21 = jnp.exp(v2 - v1)
    w1 = (1.0 / (1.0 + e21)) * p_g
    w2 = (e21 / (1.0 + e21)) * p_g
    e1 = g_idx * float(ne) + i1
    e2 = g_idx * float(ne) + i2
    slot = lax.broadcasted_iota(jnp.int32, (V7X_SUBLANES, t), 0)
    rid_ref[0] = jnp.where(slot == 0, e1, jnp.where(slot == 1, e2, 0.0)).astype(jnp.int32)
    rw_ref[0] = jnp.where(slot == 0, w1, jnp.where(slot == 1, w2, 0.0))

    @pl.when(step == last_step)
    def _final_state():
        convs_ref[0] = zbuf[0:tail, :]
        hlast_ref[0] = hc_s[...]
        cout_ref[0] = c_s[...]
        nout_ref[0] = n_s[...]
        mout_ref[0] = m_s[...]


def _mixer(z, vt, zg, x, conv0p, h0, c0, n0, m0p, p, *, batch, seq, t, chunk):
    n, d = x.shape
    heads = n0.shape[1]
    hd = n0.shape[2]
    nt = seq // t
    cc = conv0p.shape[2]
    rt = V7X_SUBLANES
    zblk = lambda width, col: pl.BlockSpec((t, width), lambda b, c: (b * nt + c, col))
    const = lambda shape: pl.BlockSpec(shape, lambda b, c: (0,) * len(shape))
    per_b = lambda shape: pl.BlockSpec((1,) + shape, lambda b, c: (b,) + (0,) * len(shape))
    tok = pl.BlockSpec((t, d), lambda b, c: (b * nt + c, 0))
    in_specs = [
        zblk(cc, 0), zblk(d, cc // d), zblk(d, cc // d + 1), zblk(d, cc // d + 2),
        pl.BlockSpec((1, d, t), lambda b, c: (b * nt + c, 0, 0)),
        zblk(GATE_LANES, 0), tok,
        per_b((rt, cc)), per_b((1, d)), per_b((heads, hd, hd)), per_b((heads, hd)),
        per_b((rt, V7X_LANES)),
        const((CONV_W, cc)), const((1, cc)), const(p["wri"].shape), const((1, d)), const((1, d)),
        const((1, d)), const((1, GATE_LANES)), const((1, d)),
        const((d, d)), const((d, d)), const((d, d)), const((1, d)),
        const((ROUTE_ROWS, d)), const((ROUTE_ROWS, d)), const((ROUTE_ROWS, 1)),
    ]
    route_blk = pl.BlockSpec((1, rt, t), lambda b, c: (b * nt + c, 0, 0))
    out_specs = (
        tok, pl.BlockSpec((t * rt, V7X_LANES), lambda b, c: (b * nt + c, 0)), route_blk, route_blk,
        per_b((rt, cc)), per_b((1, d)), per_b((heads, hd, hd)), per_b((heads, hd)),
        per_b((rt, V7X_LANES)),
    )
    out_shape = (
        jax.ShapeDtypeStruct((n, d), F32), jax.ShapeDtypeStruct((n * rt, V7X_LANES), F32),
        jax.ShapeDtypeStruct((batch * nt, rt, t), jnp.int32),
        jax.ShapeDtypeStruct((batch * nt, rt, t), F32),
        jax.ShapeDtypeStruct((batch, rt, cc), F32),
        jax.ShapeDtypeStruct((batch, 1, d), F32),
        jax.ShapeDtypeStruct((batch, heads, hd, hd), F32),
        jax.ShapeDtypeStruct((batch, heads, hd), F32),
        jax.ShapeDtypeStruct((batch, rt, V7X_LANES), F32),
    )
    scratch = [
        pltpu.VMEM((t + rt, cc), F32),
        pltpu.VMEM((t, d), F32),
        pltpu.VMEM((t, d), BF16),
        pltpu.VMEM((t, d), BF16),
        pltpu.VMEM((t, d), F32),
        pltpu.VMEM((t, d), F32),
        pltpu.VMEM((t, d), F32),
        pltpu.VMEM((t, d), F32),
        pltpu.VMEM((heads, hd, hd), F32),
        pltpu.VMEM((heads, hd), F32),
        pltpu.VMEM((rt, V7X_LANES), F32),
        pltpu.VMEM((1, d), F32),
    ]
    return pl.pallas_call(
        functools.partial(_mixer_kernel, chunk=chunk, heads=heads),
        grid=(batch, nt),
        in_specs=in_specs,
        out_specs=out_specs,
        out_shape=out_shape,
        scratch_shapes=scratch,
        compiler_params=pltpu.CompilerParams(
            dimension_semantics=("parallel", "arbitrary"),
            vmem_limit_bytes=VMEM_LIMIT_BYTES),
        name="mixer",
    )(z, z, z, z, vt, zg, x, conv0p, h0, c0, n0, m0p,
      p["conv_w"], p["conv_b"], p["wri"], p["b_r"], p["b_i"], p["lam"], p["gate_bias"], p["m_gn"],
      p["w_pa"], p["w_pb"], p["w_out"], p["g_ffn"], p["wrt_hi"], p["wrt_lo"], p["brt"])


def _expert_kernel(te_ref, nv_ref, tok_ref, tokn_ref, dst_ref, wcol_ref, xn_hbm, wgu_ref, wd_ref, o_hbm,
                   xbuf, ybuf, sem_in, sem_out):
    del te_ref
    j = pl.program_id(0)
    nv = nv_ref[0]
    rt = V7X_SUBLANES
    tm = wcol_ref.shape[0]
    d = wgu_ref.shape[1]
    de = wd_ref.shape[1]
    slot = j % 2

    def row_in(idx_ref, r, sl):
        tok = pl.multiple_of(idx_ref[0, 0, r] * rt, rt)
        return pltpu.make_async_copy(xn_hbm.at[pl.ds(tok, rt)],
                                     xbuf.at[sl, pl.ds(pl.multiple_of(r * rt, rt), rt)], sem_in.at[sl])

    def row_out(r, sl):
        dst = pl.multiple_of(dst_ref[0, 0, r] * rt, rt)
        return pltpu.make_async_copy(ybuf.at[sl, pl.ds(pl.multiple_of(r * rt, rt), rt)],
                                     o_hbm.at[pl.ds(dst, rt)], sem_out.at[sl])

    def for_rows(fn):
        def body(r, carry):
            fn(r)
            return carry
        lax.fori_loop(0, tm, body, 0, unroll=8)

    @pl.when(j == 0)
    def _zero_padding_rows():
        pad0 = o_hbm.shape[0] - 2 * tm * rt
        ybuf[1] = jnp.zeros(ybuf.shape[1:], F32)
        fills = [pltpu.make_async_copy(ybuf.at[1], o_hbm.at[pl.ds(pad0 + half * tm * rt, tm * rt)],
                                       sem_out.at[1]) for half in range(2)]
        for cp in fills:
            cp.start()
        for cp in fills:
            cp.wait()

    @pl.when(j < nv)
    def _tile():
        @pl.when(j == 0)
        def _():
            for_rows(lambda r: row_in(tok_ref, r, slot).start())

        @pl.when(j + 1 < nv)
        def _():
            for_rows(lambda r: row_in(tokn_ref, r, 1 - slot).start())

        for_rows(lambda r: row_in(tok_ref, r, slot).wait())

        x = _rows_from_tiles(xbuf.at[slot], tm, d).astype(BF16)
        gu = _dot(x, wgu_ref[0])
        hid = _silu(gu[:, :de]) * gu[:, de:]
        y = _dot(hid.astype(BF16), wd_ref[0]) * wcol_ref[...]
        _rows_to_tiles(ybuf.at[slot], y)

        for_rows(lambda r: row_out(r, slot).start())

        @pl.when(j >= 1)
        def _():
            for_rows(lambda r: row_out(r, 1 - slot).wait())

        @pl.when(j == nv - 1)
        def _():
            for_rows(lambda r: row_out(r, slot).wait())


def _experts(xn2t, e_flat, w_flat, wgu, wd, *, n_tokens, tm):
    n = n_tokens
    n_exp, de, d = wd.shape
    na = e_flat.shape[0]
    rt = V7X_SUBLANES
    p_rows = na + n_exp * tm
    ntile = p_rows // tm

    cnt = jnp.sum((e_flat[:, None] == jnp.arange(n_exp, dtype=jnp.int32)[None, :]).astype(jnp.int32), axis=0)
    pad = (-cnt) % tm
    a_idx = lax.iota(jnp.int32, na)
    pad_key = jnp.where(lax.broadcasted_iota(jnp.int32, (n_exp, tm), 1) < pad[:, None],
                        lax.broadcasted_iota(jnp.int32, (n_exp, tm), 0), n_exp).reshape(n_exp * tm)
    keys = jnp.concatenate([e_flat, pad_key])
    toks = jnp.concatenate([jnp.where(a_idx >= n, a_idx - n, a_idx), jnp.zeros((n_exp * tm,), jnp.int32)])
    dsts = jnp.concatenate([a_idx, jnp.full((n_exp * tm,), -1, jnp.int32)])
    wts = jnp.concatenate([w_flat, jnp.zeros((n_exp * tm,), F32)])
    key_s, tok_s, dst_s, w_s = lax.sort((keys, toks, dsts, wts), num_keys=1, is_stable=True)
    pos = lax.iota(jnp.int32, p_rows)
    dst_s = jnp.where(dst_s < 0, na + ((pos // tm) % 2) * tm + pos % tm, dst_s)
    tile_key = key_s[::tm]
    n_valid = jnp.sum((tile_key < n_exp).astype(jnp.int32)).reshape(1)
    tile_e = jnp.minimum(tile_key, n_exp - 1)

    idx_blk = lambda fn: pl.BlockSpec((1, 1, tm), fn, memory_space=pltpu.SMEM)
    grid_spec = pltpu.PrefetchScalarGridSpec(
        num_scalar_prefetch=2,
        grid=(ntile,),
        in_specs=[
            idx_blk(lambda j, te, nv: (j, 0, 0)),
            idx_blk(lambda j, te, nv: (jnp.minimum(j + 1, ntile - 1), 0, 0)),
            idx_blk(lambda j, te, nv: (j, 0, 0)),
            pl.BlockSpec((tm, 1), lambda j, te, nv: (j, 0)),
            pl.BlockSpec(memory_space=pl.ANY),
            pl.BlockSpec((1, d, 2 * de), lambda j, te, nv: (te[j], 0, 0)),
            pl.BlockSpec((1, de, d), lambda j, te, nv: (te[j], 0, 0)),
        ],
        out_specs=pl.BlockSpec(memory_space=pl.ANY),
        scratch_shapes=[
            pltpu.VMEM((2, tm * rt, V7X_LANES), F32), pltpu.VMEM((2, tm * rt, V7X_LANES), F32),
            pltpu.SemaphoreType.DMA((2,)), pltpu.SemaphoreType.DMA((2,)),
        ],
    )
    tok3 = tok_s.reshape(ntile, 1, tm)
    return pl.pallas_call(
        _expert_kernel,
        grid_spec=grid_spec,
        out_shape=jax.ShapeDtypeStruct(((na + 2 * tm) * rt, V7X_LANES), F32),
        compiler_params=pltpu.CompilerParams(
            dimension_semantics=("arbitrary",),
            vmem_limit_bytes=VMEM_LIMIT_BYTES),
        name="experts",
    )(tile_e, n_valid, tok3, tok3, dst_s.reshape(ntile, 1, tm), w_s.reshape(p_rows, 1), xn2t, wgu, wd)


def _final_kernel(x_ref, o0_ref, o1_ref, g_ref, y_ref):
    tm, d = x_ref.shape
    x = x_ref[...] + _rows_from_tiles(o0_ref, tm, d) + _rows_from_tiles(o1_ref, tm, d)
    y_ref[...] = _rmsnorm(x, g_ref[...])


def _final(x, moe_out, g, *, tm):
    n, d = x.shape
    nrt = n // tm
    rt = V7X_SUBLANES
    return pl.pallas_call(
        _final_kernel,
        grid=(nrt,),
        in_specs=[pl.BlockSpec((tm, d), lambda i: (i, 0)),
                  pl.BlockSpec((tm * rt, V7X_LANES), lambda i: (i, 0)),
                  pl.BlockSpec((tm * rt, V7X_LANES), lambda i: (i + nrt, 0)),
                  pl.BlockSpec((1, d), lambda i: (0, 0))],
        out_specs=pl.BlockSpec((tm, d), lambda i: (i, 0)),
        out_shape=jax.ShapeDtypeStruct((n, d), F32),
        compiler_params=pltpu.CompilerParams(dimension_semantics=("parallel",)),
        name="final_norm",
    )(x, moe_out, moe_out, g)


def _block_diag_tiles(w):
    nb, bw, _ = w.shape
    per = V7X_MXU_DIM // bw
    w4 = w.reshape(nb // per, per, bw, bw)
    eye = jnp.eye(per, dtype=w.dtype)
    return jnp.einsum("gicd,ij->gicjd", w4, eye).reshape(nb // per, per * bw, per * bw)


def _layer_params(l, g_mix, w_in, b_igate, b_fgate, conv_w, conv_b, lru_wr, lru_br, lru_wi, lru_bi,
                  lru_lambda, m_gn, w_pa, w_pb, w_out, g_ffn, w_group, b_group, w_erouter, b_erouter,
                  w_gate, w_up, w_down):
    d = w_in.shape[1]
    cc = conv_w.shape[2]
    heads = b_igate.shape[1]
    dm = m_gn.shape[1]
    assert dm == d and lru_lambda.shape[1] == d and cc == 3 * d
    w = w_in[l]
    v0, o0, gate0 = cc, cc + dm, cc + 2 * dm
    w_z = jnp.concatenate([w[:, :v0], w[:, o0:gate0], w[:, gate0 + 2 * heads:]], axis=1).astype(BF16)
    w_v = w[:, v0:o0].astype(BF16)
    w_gates = jnp.pad(w[:, gate0:gate0 + 2 * heads], ((0, 0), (0, GATE_LANES - 2 * heads))).astype(BF16)
    gate_bias = jnp.pad(jnp.concatenate([b_igate[l], b_fgate[l]]), (0, GATE_LANES - 2 * heads))[None]
    wri = jnp.concatenate([_block_diag_tiles(lru_wr[l]), _block_diag_tiles(lru_wi[l])], axis=2).astype(BF16)
    ng, _, ne = w_erouter.shape[1:]
    wr = jnp.concatenate([w_group[l], jnp.transpose(w_erouter[l], (1, 0, 2)).reshape(d, ng * ne)], axis=1)
    br = jnp.concatenate([b_group[l], b_erouter[l].reshape(ng * ne)])
    rpad = ROUTE_ROWS - wr.shape[1]
    wrt = jnp.pad(wr, ((0, 0), (0, rpad))).T
    wrt_hi = wrt.astype(BF16)
    wrt_lo = (wrt - wrt_hi.astype(F32)).astype(BF16)
    return dict(
        g_mix=g_mix[l][None], w_z=w_z, w_v=w_v, w_gates=w_gates, gate_bias=gate_bias,
        conv_w=conv_w[l], conv_b=conv_b[l][None], wri=wri,
        b_r=lru_br[l][None], b_i=lru_bi[l][None], lam=lru_lambda[l][None], m_gn=m_gn[l][None],
        w_pa=w_pa[l].astype(BF16), w_pb=w_pb[l].astype(BF16), w_out=w_out[l].astype(BF16),
        g_ffn=g_ffn[l][None], wrt_hi=wrt_hi, wrt_lo=wrt_lo, brt=jnp.pad(br, (0, rpad))[:, None],
        wgu=jnp.concatenate([w_gate[l], w_up[l]], axis=2).astype(BF16), wd=w_down[l].astype(BF16),
    )


def _tile_sizes(n, seq):
    t = min(256, seq)
    chunk = min(128, t)
    tm = min(512, n)
    tm_exp = min(512, 2 * n)
    return t, chunk, tm, tm_exp


def _trunk(x, conv0, h00, c00, n00, m00, layers, g_final):
    batch, seq, d = x.shape
    n = batch * seq
    depth = len(layers)
    t, chunk, tm, tm_exp = _tile_sizes(n, seq)
    xf = x.reshape(n, d)
    moe_out = None
    convs, hs, cs, ns, ms = [], [], [], [], []
    heads = n00.shape[2]
    rt = V7X_SUBLANES
    for l in range(depth):
        p = layers[l]
        tn = p["w_z"].shape[1] // 3
        xf, z, vt, zg = _inproj(xf, moe_out, p, tm=tm, tn=tn, t=t)
        conv0p = jnp.pad(conv0[l], ((0, 0), (rt - (CONV_W - 1), 0), (0, 0)))
        m0p = jnp.broadcast_to(
            jnp.pad(m00[l], ((0, 0), (0, rt - heads)))[:, :, None], (batch, rt, V7X_LANES))
        (xf, xn2t, rid, rw, conv_s, h_last, c_new, n_new, m_new) = _mixer(
            z, vt, zg, xf, conv0p, h00[l][:, None, :], c00[l], n00[l], m0p, p,
            batch=batch, seq=seq, t=t, chunk=chunk)
        e_flat = jnp.concatenate([rid[:, 0, :].reshape(n), rid[:, 1, :].reshape(n)])
        w_flat = jnp.concatenate([rw[:, 0, :].reshape(n), rw[:, 1, :].reshape(n)])
        moe_out = _experts(xn2t, e_flat, w_flat, p["wgu"], p["wd"], n_tokens=n, tm=tm_exp)
        convs.append(conv_s[:, rt - (CONV_W - 1):, :])
        hs.append(h_last[:, 0, :])
        cs.append(c_new)
        ns.append(n_new)
        ms.append(m_new[:, :heads, 0])
    y = _final(xf, moe_out, g_final[None], tm=tm).reshape(batch, seq, d)
    return y, jnp.stack(convs), jnp.stack(hs), jnp.stack(cs), jnp.stack(ns), jnp.stack(ms)


def kernel(x_prompt, x_sample, state_conv, state_lru, state_mlstm_C, state_mlstm_n, state_mlstm_m,
           g_mix, w_in, b_igate, b_fgate, conv_w, conv_b, lru_wr, lru_br, lru_wi, lru_bi, lru_lambda,
           m_gn, w_pa, w_pb, w_out, g_ffn, w_group, b_group, w_erouter, b_erouter,
           w_gate, w_up, w_down, g_final):
    depth = w_in.shape[0]
    layers = [
        _layer_params(l, g_mix, w_in, b_igate, b_fgate, conv_w, conv_b, lru_wr, lru_br, lru_wi, lru_bi,
                      lru_lambda, m_gn, w_pa, w_pb, w_out, g_ffn, w_group, b_group, w_erouter,
                      b_erouter, w_gate, w_up, w_down)
        for l in range(depth)]
    dt = x_prompt.dtype
    bp = x_prompt.shape[0]
    zeros_like_state = lambda s: jnp.zeros((depth, bp) + s.shape[2:], dt)
    y_s, s_conv, s_lru, s_c, s_n, s_m = _trunk(
        x_sample, state_conv, state_lru, state_mlstm_C, state_mlstm_n, state_mlstm_m,
        layers, g_final)
    y_p, p_conv, p_lru, p_c, p_n, p_m = _trunk(
        x_prompt, zeros_like_state(state_conv), zeros_like_state(state_lru),
        zeros_like_state(state_mlstm_C), zeros_like_state(state_mlstm_n),
        zeros_like_state(state_mlstm_m), layers, g_final)
    return (y_p, y_s, p_conv, p_lru, p_c, p_n, p_m, s_conv, s_lru, s_c, s_n, s_m)
```

```python
import functools

import jax
import jax.numpy as jnp
from jax import lax
from jax.experimental import pallas as pl
from jax.experimental.pallas import tpu as pltpu

F32 = jnp.float32
BF16 = jnp.bfloat16

EPS = 1e-6
LRU_C = 8.0
CONV_W = 4
N_GROUPS = 4
EXP_PER_GROUP = 4

V7X_VMEM_BYTES = 64 * 2**20
V7X_LANES = 128
V7X_SUBLANES = 8
V7X_MXU_DIM = 256

VMEM_LIMIT_BYTES = V7X_VMEM_BYTES - 8 * 2**20

ROUTE_ROWS = 32
GATE_LANES = V7X_LANES


def _rmsnorm(x, g):
    return (x * lax.rsqrt(jnp.mean(x * x, axis=-1, keepdims=True) + EPS)) * g


def _sigmoid(x):
    return 0.5 * jnp.tanh(0.5 * x) + 0.5


def _silu(x):
    h = 0.5 * x
    return h * jnp.tanh(h) + h


def _log_sigmoid(x):
    return jnp.minimum(x, 0.0) - jnp.log1p(jnp.exp(-jnp.abs(x)))


def _softplus(x):
    return jnp.maximum(x, 0.0) + jnp.log1p(jnp.exp(-jnp.abs(x)))


def _dot(a, b):
    return jnp.dot(a, b, preferred_element_type=F32)


def _dot_nt(a, b, precision=None):
    return lax.dot_general(a, b, (((1,), (1,)), ((), ())), precision=precision,
                           preferred_element_type=F32)


def _split_bf16(x):
    hi = x.astype(BF16)
    return hi, (x - hi.astype(F32)).astype(BF16)


def _rows_from_tiles(ref, n_rows, d):
    return jnp.concatenate(
        [ref[pl.ds(j, n_rows, stride=V7X_SUBLANES), :] for j in range(d // V7X_LANES)], axis=1)


def _rows_to_tiles(ref, x):
    n_rows, d = x.shape
    for j in range(d // V7X_LANES):
        ref[pl.ds(j, n_rows, stride=V7X_SUBLANES), :] = x[:, j * V7X_LANES:(j + 1) * V7X_LANES]


def _inproj_kernel(*refs, with_moe, t):
    if with_moe:
        x_ref, o0_ref, o1_ref, g_ref, w_ref, wv_ref, wg_ref, xres_ref, z_ref, vt_ref, zg_ref, xn_s = refs
    else:
        x_ref, g_ref, w_ref, wv_ref, wg_ref, z_ref, vt_ref, zg_ref, xn_s = refs
    tm, d = x_ref.shape
    j = pl.program_id(1)

    @pl.when(j == 0)
    def _():
        x = x_ref[...]
        if with_moe:
            x = x + _rows_from_tiles(o0_ref, tm, d) + _rows_from_tiles(o1_ref, tm, d)
            xres_ref[...] = x
        xn_s[...] = _rmsnorm(x, g_ref[...]).astype(BF16)
        zg_ref[...] = _dot(xn_s[...], wg_ref[...])
        v = _dot(xn_s[...], wv_ref[...])
        tp = -(-t // V7X_LANES) * V7X_LANES
        for q in range(tm // t):
            vq = v[q * t:(q + 1) * t, :]
            if tp != t:
                vq = jnp.concatenate([vq, jnp.zeros((tp - t, d), F32)], axis=0)
            vt_ref[q] = vq.T[:, :t]

    @pl.when(j > 0)
    def _():
        z_ref[...] = _dot(xn_s[...], w_ref[...])


def _inproj(x, moe_out, p, *, tm, tn, t):
    n, d = x.shape
    zc = p["w_z"].shape[1]
    nrt = n // tm
    rt = V7X_SUBLANES
    parts = [x]
    part_specs = [pl.BlockSpec((tm, d), lambda i, j: (i, 0))]
    if moe_out is not None:
        parts += [moe_out, moe_out]
        part_specs += [pl.BlockSpec((tm * rt, V7X_LANES), lambda i, j: (i, 0)),
                       pl.BlockSpec((tm * rt, V7X_LANES), lambda i, j: (i + nrt, 0))]
    out_shape = [jax.ShapeDtypeStruct((n, zc), F32),
                 jax.ShapeDtypeStruct((n // t, d, t), F32),
                 jax.ShapeDtypeStruct((n, GATE_LANES), F32)]
    zcol = lambda j: jnp.maximum(j - 1, 0)
    out_specs = [pl.BlockSpec((tm, tn), lambda i, j: (i, zcol(j))),
                 pl.BlockSpec((tm // t, d, t), lambda i, j: (i, 0, 0)),
                 pl.BlockSpec((tm, GATE_LANES), lambda i, j: (i, 0))]
    if moe_out is not None:
        out_shape = [jax.ShapeDtypeStruct((n, d), F32)] + out_shape
        out_specs = [pl.BlockSpec((tm, d), lambda i, j: (i, 0))] + out_specs
    res = pl.pallas_call(
        functools.partial(_inproj_kernel, with_moe=moe_out is not None, t=t),
        grid=(nrt, zc // tn + 1),
        in_specs=part_specs + [pl.BlockSpec((1, d), lambda i, j: (0, 0)),
                               pl.BlockSpec((d, tn), lambda i, j: (0, zcol(j))),
                               pl.BlockSpec((d, d), lambda i, j: (0, 0)),
                               pl.BlockSpec((d, GATE_LANES), lambda i, j: (0, 0))],
        out_specs=out_specs,
        out_shape=out_shape,
        scratch_shapes=[pltpu.VMEM((tm, d), BF16)],
        compiler_params=pltpu.CompilerParams(
            dimension_semantics=("parallel", "arbitrary"),
            vmem_limit_bytes=VMEM_LIMIT_BYTES),
        name="inproj",
    )(*parts, p["g_mix"], p["w_z"], p["w_v"], p["w_gates"])
    if moe_out is not None:
        return res
    return [x] + list(res)


def _mixer_kernel(zc_ref, zo_ref, zga_ref, zgb_ref, vt_ref, zg_ref, x_ref,
                  conv0_ref, h0_ref, c0_ref, n0_ref, m0_ref,
                  cw_ref, cb_ref, wri_ref, br_ref, bi_ref, lam_ref, gbias_ref, mgn_ref,
                  wpa_ref, wpb_ref, wout_ref, gffn_ref, wrh_ref, wrl_ref, brt_ref,
                  xnew_ref, xn2_ref, rid_ref, rw_ref,
                  convs_ref, hlast_ref, cout_ref, nout_ref, mout_ref,
                  zbuf, xl_s, q_s, k_s, a_s, u_s, h_s, hm_s, c_s, n_s, m_s, hc_s,
                  *, chunk, heads):
    t = x_ref.shape[0]
    d = x_ref.shape[1]
    hd = d // heads
    step = pl.program_id(1)
    last_step = pl.num_programs(1) - 1
    tail = V7X_SUBLANES
    first_tap = tail - (CONV_W - 1)

    @pl.when(step == 0)
    def _init():
        zbuf[0:tail, :] = conv0_ref[0]
        hc_s[...] = h0_ref[0]
        c_s[...] = c0_ref[0]
        n_s[...] = n0_ref[0]
        m_s[...] = m0_ref[0]

    zbuf[tail:tail + t, :] = zc_ref[...]

    def conv_cols(lo, hi):
        acc = zbuf[first_tap:first_tap + t, lo:hi] * cw_ref[0:1, lo:hi]
        for tap in range(1, CONV_W):
            acc = acc + zbuf[first_tap + tap:first_tap + tap + t, lo:hi] * cw_ref[tap:tap + 1, lo:hi]
        return acc + cb_ref[:, lo:hi]

    q_s[...] = _silu(conv_cols(d, 2 * d)).astype(BF16)
    k_s[...] = (_silu(conv_cols(2 * d, 3 * d)) * (hd ** -0.5)).astype(BF16)

    tc = chunk
    gates = zg_ref[...] + gbias_ref[...]
    lf = _log_sigmoid(gates)
    tri_r = lax.broadcasted_iota(jnp.int32, (tc, tc), 0)
    tri_c = lax.broadcasted_iota(jnp.int32, (tc, tc), 1)
    causal = tri_c <= tri_r
    tri = causal.astype(F32)
    pick = (lax.broadcasted_iota(jnp.int32, (V7X_SUBLANES, GATE_LANES), 0) ==
            lax.broadcasted_iota(jnp.int32, (V7X_SUBLANES, GATE_LANES), 1)).astype(F32)
    for j in range(t // tc):
        r0, r1 = j * tc, (j + 1) * tc
        bcum = jnp.dot(tri, lf[r0:r1], precision=lax.Precision.HIGHEST,
                       preferred_element_type=F32)
        b_al = pltpu.roll(bcum, GATE_LANES - heads, 1)
        g_al = gates[r0:r1] - b_al
        g_rows = _dot_nt(pick, g_al, precision=lax.Precision.HIGHEST)
        for h in range(heads):
            c0, c1 = h * hd, (h + 1) * hd
            qh = q_s[r0:r1, c0:c1]
            kh = k_s[r0:r1, c0:c1]
            vth = vt_ref[0, c0:c1, r0:r1].astype(BF16)
            bcol = b_al[:, h:h + 1]
            gcol = g_al[:, h:h + 1]
            m_prev = m_s[h:h + 1, 0:1]
            dlog = jnp.where(causal, bcol + g_rows[h:h + 1, :], -jnp.inf)
            m_inter = bcol + m_prev
            m_t = jnp.maximum(m_inter, jnp.max(dlog, axis=-1, keepdims=True))
            s = _dot_nt(qh, kh) * jnp.exp(dlog - m_t)
            inter = jnp.exp(m_inter - m_t)
            qc_ = _dot_nt(qh, c_s[h].astype(BF16))
            num = _dot_nt(s.astype(BF16), vth) + inter * qc_
            qn = jnp.sum(qh.astype(F32) * n_s[h:h + 1, :], axis=-1, keepdims=True)
            den = jnp.sum(s, axis=-1, keepdims=True) + inter * qn
            hm_s[r0:r1, c0:c1] = num / jnp.maximum(jnp.abs(den), jnp.exp(-m_t))
            b_end = bcol[tc - 1:tc, :]
            m_new = m_t[tc - 1:tc, :]
            wk = jnp.exp(b_end + gcol - m_new)
            decay = jnp.exp(b_end + m_prev - m_new)
            kw = wk * kh.astype(F32)
            c_s[h] = decay * c_s[h] + _dot(vth, kw.astype(BF16))
            n_s[h:h + 1, :] = decay * n_s[h:h + 1, :] + jnp.sum(kw, axis=0, keepdims=True)
            m_s[h:h + 1, :] = jnp.broadcast_to(m_new, (1, V7X_LANES))

    xl_s[...] = conv_cols(0, d)
    zbuf[0:tail, :] = zbuf[t:t + tail, :]
    gw = V7X_MXU_DIM
    sp = _softplus(-lam_ref[...])
    for g in range(d // gw):
        lo, hi = g * gw, (g + 1) * gw
        xg = xl_s[:, lo:hi]
        ri = _dot(xg.astype(BF16), wri_ref[g])
        r = _sigmoid(ri[:, :gw] + br_ref[:, lo:hi])
        ig = _sigmoid(ri[:, gw:] + bi_ref[:, lo:hi])
        log_a = (-LRU_C * r) * sp[:, lo:hi]
        a = jnp.exp(log_a)
        mult = jnp.sqrt(-jnp.tanh(log_a) * (a * a + 1.0))
        a_s[:, lo:hi] = a
        u_s[:, lo:hi] = mult * (ig * xg)

    rows = lax.broadcasted_iota(jnp.int32, (V7X_SUBLANES, d), 0)
    shifts = (1, 2, 4)
    masks = [rows >= s for s in shifts]
    hc = hc_s[...]
    for gi in range(t // V7X_SUBLANES):
        r0 = gi * V7X_SUBLANES
        a = a_s[r0:r0 + V7X_SUBLANES, :]
        u = u_s[r0:r0 + V7X_SUBLANES, :]
        for s, mk in zip(shifts, masks):
            a_sh = jnp.where(mk, pltpu.roll(a, s, 0), 1.0)
            u_sh = jnp.where(mk, pltpu.roll(u, s, 0), 0.0)
            u = a * u_sh + u
            a = a * a_sh
        hg = u + a * hc
        h_s[r0:r0 + V7X_SUBLANES, :] = hg
        hc = hg[V7X_SUBLANES - 1:V7X_SUBLANES, :]
    hc_s[...] = hc

    for h in range(heads):
        c0, c1 = h * hd, (h + 1) * hd
        seg = hm_s[:, c0:c1]
        seg = seg * lax.rsqrt(jnp.mean(seg * seg, axis=-1, keepdims=True) + EPS)
        hm_s[:, c0:c1] = (seg * mgn_ref[:, c0:c1]) * _sigmoid(zo_ref[:, c0:c1])
    pa = _dot(h_s[...].astype(BF16), wpa_ref[...])
    pb = _dot(hm_s[...].astype(BF16), wpb_ref[...])
    merged = _sigmoid(zga_ref[...]) * pa + _sigmoid(zgb_ref[...]) * pb
    xnew = x_ref[...] + _dot(merged.astype(BF16), wout_ref[...])
    xnew_ref[...] = xnew
    xn2 = _rmsnorm(xnew, gffn_ref[...])
    _rows_to_tiles(xn2_ref, xn2)

    xh, xl = _split_bf16(xn2)
    logit = (_dot_nt(wrh_ref[...], xh) + _dot_nt(wrh_ref[...], xl) + _dot_nt(wrl_ref[...], xh)
             + brt_ref[...])
    ng, ne = N_GROUPS, EXP_PER_GROUP
    gl = logit[0:ng]
    gidx = lax.broadcasted_iota(jnp.int32, (ng, t), 0).astype(F32)
    idx = lax.broadcasted_iota(jnp.int32, (ne, t), 0).astype(F32)
    gmax = jnp.max(gl, axis=0, keepdims=True)
    g_idx = jnp.min(jnp.where(gl == gmax, gidx, float(ng)), axis=0, keepdims=True)
    p_g = 1.0 / jnp.sum(jnp.exp(gl - gmax), axis=0, keepdims=True)
    el = jnp.zeros((ne, t), F32)
    for g in range(ng):
        el = jnp.where(g_idx == float(g), logit[ng + g * ne:ng + (g + 1) * ne], el)
    v1 = jnp.max(el, axis=0, keepdims=True)
    i1 = jnp.min(jnp.where(el == v1, idx, float(ne)), axis=0, keepdims=True)
    rest = jnp.where(idx == i1, -jnp.inf, el)
    v2 = jnp.max(rest, axis=0, keepdims=True)
    i2 = jnp.min(jnp.where(rest == v2, idx, float(ne)), axis=0, keepdims=True)
    e21 = jnp.exp(v2 - v1)
    w1 = (1.0 / (1.0 + e21)) * p_g
    w2 = (e21 / (1.0 + e21)) * p_g
    e1 = g_idx * float(ne) + i1
    e2 = g_idx * float(ne) + i2
    slot = lax.broadcasted_iota(jnp.int32, (V7X_SUBLANES, t), 0)
    rid_ref[0] = jnp.where(slot == 0, e1, jnp.where(slot == 1, e2, 0.0)).astype(jnp.int32)
    rw_ref[0] = jnp.where(slot == 0, w1, jnp.where(slot == 1, w2, 0.0))

    @pl.when(step == last_step)
    def _final_state():
        convs_ref[0] = zbuf[0:tail, :]
        hlast_ref[0] = hc_s[...]
        cout_ref[0] = c_s[...]
        nout_ref[0] = n_s[...]
        mout_ref[0] = m_s[...]


def _mixer(z, vt, zg, x, conv0p, h0, c0, n0, m0p, p, *, batch, seq, t, chunk):
    n, d = x.shape
    heads = n0.shape[1]
    hd = n0.shape[2]
    nt = seq // t
    cc = conv0p.shape[2]
    rt = V7X_SUBLANES
    zblk = lambda width, col: pl.BlockSpec((t, width), lambda b, c: (b * nt + c, col))
    const = lambda shape: pl.BlockSpec(shape, lambda b, c: (0,) * len(shape))
    per_b = lambda shape: pl.BlockSpec((1,) + shape, lambda b, c: (b,) + (0,) * len(shape))
    tok = pl.BlockSpec((t, d), lambda b, c: (b * nt + c, 0))
    in_specs = [
        zblk(cc, 0), zblk(d, cc // d), zblk(d, cc // d + 1), zblk(d, cc // d + 2),
        pl.BlockSpec((1, d, t), lambda b, c: (b * nt + c, 0, 0)),
        zblk(GATE_LANES, 0), tok,
        per_b((rt, cc)), per_b((1, d)), per_b((heads, hd, hd)), per_b((heads, hd)),
        per_b((rt, V7X_LANES)),
        const((CONV_W, cc)), const((1, cc)), const(p["wri"].shape), const((1, d)), const((1, d)),
        const((1, d)), const((1, GATE_LANES)), const((1, d)),
        const((d, d)), const((d, d)), const((d, d)), const((1, d)),
        const((ROUTE_ROWS, d)), const((ROUTE_ROWS, d)), const((ROUTE_ROWS, 1)),
    ]
    route_blk = pl.BlockSpec((1, rt, t), lambda b, c: (b * nt + c, 0, 0))
    out_specs = (
        tok, pl.BlockSpec((t * rt, V7X_LANES), lambda b, c: (b * nt + c, 0)), route_blk, route_blk,
        per_b((rt, cc)), per_b((1, d)), per_b((heads, hd, hd)), per_b((heads, hd)),
        per_b((rt, V7X_LANES)),
    )
    out_shape = (
        jax.ShapeDtypeStruct((n, d), F32), jax.ShapeDtypeStruct((n * rt, V7X_LANES), F32),
        jax.ShapeDtypeStruct((batch * nt, rt, t), jnp.int32),
        jax.ShapeDtypeStruct((batch * nt, rt, t), F32),
        jax.ShapeDtypeStruct((batch, rt, cc), F32),
        jax.ShapeDtypeStruct((batch, 1, d), F32),
        jax.ShapeDtypeStruct((batch, heads, hd, hd), F32),
        jax.ShapeDtypeStruct((batch, heads, hd), F32),
        jax.ShapeDtypeStruct((batch, rt, V7X_LANES), F32),
    )
    scratch = [
        pltpu.VMEM((t + rt, cc), F32),
        pltpu.VMEM((t, d), F32),
        pltpu.VMEM((t, d), BF16),
        pltpu.VMEM((t, d), BF16),
        pltpu.VMEM((t, d), F32),
        pltpu.VMEM((t, d), F32),
        pltpu.VMEM((t, d), F32),
        pltpu.VMEM((t, d), F32),
        pltpu.VMEM((heads, hd, hd), F32),
        pltpu.VMEM((heads, hd), F32),
        pltpu.VMEM((rt, V7X_LANES), F32),
        pltpu.VMEM((1, d), F32),
    ]
    return pl.pallas_call(
        functools.partial(_mixer_kernel, chunk=chunk, heads=heads),
        grid=(batch, nt),
        in_specs=in_specs,
        out_specs=out_specs,
        out_shape=out_shape,
        scratch_shapes=scratch,
        compiler_params=pltpu.CompilerParams(
            dimension_semantics=("parallel", "arbitrary"),
            vmem_limit_bytes=VMEM_LIMIT_BYTES),
        name="mixer",
    )(z, z, z, z, vt, zg, x, conv0p, h0, c0, n0, m0p,
      p["conv_w"], p["conv_b"], p["wri"], p["b_r"], p["b_i"], p["lam"], p["gate_bias"], p["m_gn"],
      p["w_pa"], p["w_pb"], p["w_out"], p["g_ffn"], p["wrt_hi"], p["wrt_lo"], p["brt"])


def _expert_kernel(te_ref, nv_ref, tok_ref, tokn_ref, dst_ref, wcol_ref, xn_hbm, wgu_ref, wd_ref, o_hbm,
                   xbuf, ybuf, sem_in, sem_out):
    del te_ref
    j = pl.program_id(0)
    nv = nv_ref[0]
    rt = V7X_SUBLANES
    tm = wcol_ref.shape[0]
    d = wgu_ref.shape[1]
    de = wd_ref.shape[1]
    slot = j % 2

    def row_in(idx_ref, r, sl):
        tok = pl.multiple_of(idx_ref[0, 0, r] * rt, rt)
        return pltpu.make_async_copy(xn_hbm.at[pl.ds(tok, rt)],
                                     xbuf.at[sl, pl.ds(pl.multiple_of(r * rt, rt), rt)], sem_in.at[sl])

    def row_out(r, sl):
        dst = pl.multiple_of(dst_ref[0, 0, r] * rt, rt)
        return pltpu.make_async_copy(ybuf.at[sl, pl.ds(pl.multiple_of(r * rt, rt), rt)],
                                     o_hbm.at[pl.ds(dst, rt)], sem_out.at[sl])

    def for_rows(fn):
        def body(r, carry):
            fn(r)
            return carry
        lax.fori_loop(0, tm, body, 0, unroll=8)

    @pl.when(j == 0)
    def _zero_padding_rows():
        pad0 = o_hbm.shape[0] - 2 * tm * rt
        ybuf[1] = jnp.zeros(ybuf.shape[1:], F32)
        fills = [pltpu.make_async_copy(ybuf.at[1], o_hbm.at[pl.ds(pad0 + half * tm * rt, tm * rt)],
                                       sem_out.at[1]) for half in range(2)]
        for cp in fills:
            cp.start()
        for cp in fills:
            cp.wait()

    @pl.when(j < nv)
    def _tile():
        @pl.when(j == 0)
        def _():
            for_rows(lambda r: row_in(tok_ref, r, slot).start())

        @pl.when(j + 1 < nv)
        def _():
            for_rows(lambda r: row_in(tokn_ref, r, 1 - slot).start())

        for_rows(lambda r: row_in(tok_ref, r, slot).wait())

        x = _rows_from_tiles(xbuf.at[slot], tm, d).astype(BF16)
        gu = _dot(x, wgu_ref[0])
        hid = _silu(gu[:, :de]) * gu[:, de:]
        y = _dot(hid.astype(BF16), wd_ref[0]) * wcol_ref[...]
        _rows_to_tiles(ybuf.at[slot], y)

        for_rows(lambda r: row_out(r, slot).start())

        @pl.when(j >= 1)
        def _():
            for_rows(lambda r: row_out(r, 1 - slot).wait())

        @pl.when(j == nv - 1)
        def _():
            for_rows(lambda r: row_out(r, slot).wait())


def _experts(xn2t, e_flat, w_flat, wgu, wd, *, n_tokens, tm):
    n = n_tokens
    n_exp, de, d = wd.shape
    na = e_flat.shape[0]
    rt = V7X_SUBLANES
    p_rows = na + n_exp * tm
    ntile = p_rows // tm

    cnt = jnp.sum((e_flat[:, None] == jnp.arange(n_exp, dtype=jnp.int32)[None, :]).astype(jnp.int32), axis=0)
    pad = (-cnt) % tm
    pad_key = jnp.where(lax.broadcasted_iota(jnp.int32, (n_exp, tm), 1) < pad[:, None],
                        lax.broadcasted_iota(jnp.int32, (n_exp, tm), 0), n_exp).reshape(n_exp * tm)
    keys = jnp.concatenate([e_flat, pad_key])
    wts = jnp.concatenate([w_flat, jnp.zeros((n_exp * tm,), F32)])
    shift = (p_rows - 1).bit_length()
    assert (n_exp + 1) << shift < 2**31
    pos = lax.iota(jnp.int32, p_rows)
    packed_s, w_s = lax.sort((keys * (1 << shift) + pos, wts), num_keys=1)
    key_s = packed_s >> shift
    src_s = packed_s & ((1 << shift) - 1)
    is_pad = src_s >= na
    tok_s = jnp.where(is_pad, 0, jnp.where(src_s >= n, src_s - n, src_s))
    dst_s = jnp.where(is_pad, na + ((pos // tm) % 2) * tm + pos % tm, src_s)
    tile_key = key_s[::tm]
    n_valid = jnp.sum((tile_key < n_exp).astype(jnp.int32)).reshape(1)
    tile_e = jnp.minimum(tile_key, n_exp - 1)

    idx_blk = lambda fn: pl.BlockSpec((1, 1, tm), fn, memory_space=pltpu.SMEM)
    grid_spec = pltpu.PrefetchScalarGridSpec(
        num_scalar_prefetch=2,
        grid=(ntile,),
        in_specs=[
            idx_blk(lambda j, te, nv: (j, 0, 0)),
            idx_blk(lambda j, te, nv: (jnp.minimum(j + 1, ntile - 1), 0, 0)),
            idx_blk(lambda j, te, nv: (j, 0, 0)),
            pl.BlockSpec((tm, 1), lambda j, te, nv: (j, 0)),
            pl.BlockSpec(memory_space=pl.ANY),
            pl.BlockSpec((1, d, 2 * de), lambda j, te, nv: (te[j], 0, 0)),
            pl.BlockSpec((1, de, d), lambda j, te, nv: (te[j], 0, 0)),
        ],
        out_specs=pl.BlockSpec(memory_space=pl.ANY),
        scratch_shapes=[
            pltpu.VMEM((2, tm * rt, V7X_LANES), F32), pltpu.VMEM((2, tm * rt, V7X_LANES), F32),
            pltpu.SemaphoreType.DMA((2,)), pltpu.SemaphoreType.DMA((2,)),
        ],
    )
    tok3 = tok_s.reshape(ntile, 1, tm)
    return pl.pallas_call(
        _expert_kernel,
        grid_spec=grid_spec,
        out_shape=jax.ShapeDtypeStruct(((na + 2 * tm) * rt, V7X_LANES), F32),
        compiler_params=pltpu.CompilerParams(
            dimension_semantics=("arbitrary",),
            vmem_limit_bytes=VMEM_LIMIT_BYTES),
        name="experts",
    )(tile_e, n_valid, tok3, tok3, dst_s.reshape(ntile, 1, tm), w_s.reshape(p_rows, 1), xn2t, wgu, wd)


def _final_kernel(x_ref, o0_ref, o1_ref, g_ref, y_ref):
    tm, d = x_ref.shape
    x = x_ref[...] + _rows_from_tiles(o0_ref, tm, d) + _rows_from_tiles(o1_ref, tm, d)
    y_ref[...] = _rmsnorm(x, g_ref[...])


def _final(x, moe_out, g, *, tm):
    n, d = x.shape
    nrt = n // tm
    rt = V7X_SUBLANES
    return pl.pallas_call(
        _final_kernel,
        grid=(nrt,),
        in_specs=[pl.BlockSpec((tm, d), lambda i: (i, 0)),
                  pl.BlockSpec((tm * rt, V7X_LANES), lambda i: (i, 0)),
                  pl.BlockSpec((tm * rt, V7X_LANES), lambda i: (i + nrt, 0)),
                  pl.BlockSpec((1, d), lambda i: (0, 0))],
        out_specs=pl.BlockSpec((tm, d), lambda i: (i, 0)),
        out_shape=jax.ShapeDtypeStruct((n, d), F32),
        compiler_params=pltpu.CompilerParams(dimension_semantics=("parallel",)),
        name="final_norm",
    )(x, moe_out, moe_out, g)


def _block_diag_tiles(w):
    nb, bw, _ = w.shape
    per = V7X_MXU_DIM // bw
    w4 = w.reshape(nb // per, per, bw, bw)
    eye = jnp.eye(per, dtype=w.dtype)
    return jnp.einsum("gicd,ij->gicjd", w4, eye).reshape(nb // per, per * bw, per * bw)


def _layer_params(l, g_mix, w_in, b_igate, b_fgate, conv_w, conv_b, lru_wr, lru_br, lru_wi, lru_bi,
                  lru_lambda, m_gn, w_pa, w_pb, w_out, g_ffn, w_group, b_group, w_erouter, b_erouter,
                  w_gate, w_up, w_down):
    d = w_in.shape[1]
    cc = conv_w.shape[2]
    heads = b_igate.shape[1]
    dm = m_gn.shape[1]
    assert dm == d and lru_lambda.shape[1] == d and cc == 3 * d
    w = w_in[l]
    v0, o0, gate0 = cc, cc + dm, cc + 2 * dm
    w_z = jnp.concatenate([w[:, :v0], w[:, o0:gate0], w[:, gate0 + 2 * heads:]], axis=1).astype(BF16)
    w_v = w[:, v0:o0].astype(BF16)
    w_gates = jnp.pad(w[:, gate0:gate0 + 2 * heads], ((0, 0), (0, GATE_LANES - 2 * heads))).astype(BF16)
    gate_bias = jnp.pad(jnp.concatenate([b_igate[l], b_fgate[l]]), (0, GATE_LANES - 2 * heads))[None]
    wri = jnp.concatenate([_block_diag_tiles(lru_wr[l]), _block_diag_tiles(lru_wi[l])], axis=2).astype(BF16)
    ng, _, ne = w_erouter.shape[1:]
    wr = jnp.concatenate([w_group[l], jnp.transpose(w_erouter[l], (1, 0, 2)).reshape(d, ng * ne)], axis=1)
    br = jnp.concatenate([b_group[l], b_erouter[l].reshape(ng * ne)])
    rpad = ROUTE_ROWS - wr.shape[1]
    wrt = jnp.pad(wr, ((0, 0), (0, rpad))).T
    wrt_hi = wrt.astype(BF16)
    wrt_lo = (wrt - wrt_hi.astype(F32)).astype(BF16)
    return dict(
        g_mix=g_mix[l][None], w_z=w_z, w_v=w_v, w_gates=w_gates, gate_bias=gate_bias,
        conv_w=conv_w[l], conv_b=conv_b[l][None], wri=wri,
        b_r=lru_br[l][None], b_i=lru_bi[l][None], lam=lru_lambda[l][None], m_gn=m_gn[l][None],
        w_pa=w_pa[l].astype(BF16), w_pb=w_pb[l].astype(BF16), w_out=w_out[l].astype(BF16),
        g_ffn=g_ffn[l][None], wrt_hi=wrt_hi, wrt_lo=wrt_lo, brt=jnp.pad(br, (0, rpad))[:, None],
        wgu=jnp.concatenate([w_gate[l], w_up[l]], axis=2).astype(BF16), wd=w_down[l].astype(BF16),
    )


def _tile_sizes(n, seq, n_exp):
    t = min(256, seq)
    chunk = min(128, t)
    tm_first = min(1024, n)
    tm = min(512, n)
    tm_exp = 512 if 2 * n >= 8 * n_exp * 512 else 128
    return t, chunk, tm_first, tm, tm_exp


def _trunk(x, conv0, h00, c00, n00, m00, layers, g_final):
    batch, seq, d = x.shape
    n = batch * seq
    depth = len(layers)
    t, chunk, tm_first, tm, tm_exp = _tile_sizes(n, seq, layers[0]["wd"].shape[0])
    xf = x.reshape(n, d)
    moe_out = None
    convs, hs, cs, ns, ms = [], [], [], [], []
    heads = n00.shape[2]
    rt = V7X_SUBLANES
    for l in range(depth):
        p = layers[l]
        if moe_out is None:
            tm_in, col_blocks = tm_first, 6
        else:
            tm_in, col_blocks = tm, 3
        xf, z, vt, zg = _inproj(xf, moe_out, p, tm=tm_in, tn=p["w_z"].shape[1] // col_blocks, t=t)
        conv0p = jnp.pad(conv0[l], ((0, 0), (rt - (CONV_W - 1), 0), (0, 0)))
        m0p = jnp.broadcast_to(
            jnp.pad(m00[l], ((0, 0), (0, rt - heads)))[:, :, None], (batch, rt, V7X_LANES))
        (xf, xn2t, rid, rw, conv_s, h_last, c_new, n_new, m_new) = _mixer(
            z, vt, zg, xf, conv0p, h00[l][:, None, :], c00[l], n00[l], m0p, p,
            batch=batch, seq=seq, t=t, chunk=chunk)
        e_flat = jnp.concatenate([rid[:, 0, :].reshape(n), rid[:, 1, :].reshape(n)])
        w_flat = jnp.concatenate([rw[:, 0, :].reshape(n), rw[:, 1, :].reshape(n)])
        moe_out = _experts(xn2t, e_flat, w_flat, p["wgu"], p["wd"], n_tokens=n, tm=tm_exp)
        convs.append(conv_s[:, rt - (CONV_W - 1):, :])
        hs.append(h_last[:, 0, :])
        cs.append(c_new)
        ns.append(n_new)
        ms.append(m_new[:, :heads, 0])
    y = _final(xf, moe_out, g_final[None], tm=tm).reshape(batch, seq, d)
    return y, jnp.stack(convs), jnp.stack(hs), jnp.stack(cs), jnp.stack(ns), jnp.stack(ms)


def kernel(x_prompt, x_sample, state_conv, state_lru, state_mlstm_C, state_mlstm_n, state_mlstm_m,
           g_mix, w_in, b_igate, b_fgate, conv_w, conv_b, lru_wr, lru_br, lru_wi, lru_bi, lru_lambda,
           m_gn, w_pa, w_pb, w_out, g_ffn, w_group, b_group, w_erouter, b_erouter,
           w_gate, w_up, w_down, g_final):
    depth = w_in.shape[0]
    layers = [
        _layer_params(l, g_mix, w_in, b_igate, b_fgate, conv_w, conv_b, lru_wr, lru_br, lru_wi, lru_bi,
                      lru_lambda, m_gn, w_pa, w_pb, w_out, g_ffn, w_group, b_group, w_erouter,
                      b_erouter, w_gate, w_up, w_down)
        for l in range(depth)]
    dt = x_prompt.dtype
    bp = x_prompt.shape[0]
    zeros_like_state = lambda s: jnp.zeros((depth, bp) + s.shape[2:], dt)
    y_s, s_conv, s_lru, s_c, s_n, s_m = _trunk(
        x_sample, state_conv, state_lru, state_mlstm_C, state_mlstm_n, state_mlstm_m,
        layers, g_final)
    y_p, p_conv, p_lru, p_c, p_n, p_m = _trunk(
        x_prompt, zeros_like_state(state_conv), zeros_like_state(state_lru),
        zeros_like_state(state_mlstm_C), zeros_like_state(state_mlstm_n),
        zeros_like_state(state_mlstm_m), layers, g_final)
    return (y_p, y_s, p_conv, p_lru, p_c, p_n, p_m, s_conv, s_lru, s_c, s_n, s_m)
```

```python
import functools

import jax
import jax.numpy as jnp
from jax import lax
from jax.experimental import pallas as pl
from jax.experimental.pallas import tpu as pltpu

F32 = jnp.float32
BF16 = jnp.bfloat16

EPS = 1e-6
LRU_C = 8.0
CONV_W = 4
N_GROUPS = 4
EXP_PER_GROUP = 4

V7X_VMEM_BYTES = 64 * 2**20
V7X_LANES = 128
V7X_SUBLANES = 8
V7X_MXU_DIM = 256

VMEM_LIMIT_BYTES = V7X_VMEM_BYTES - 8 * 2**20

ROUTE_ROWS = 32
GATE_LANES = V7X_LANES


def _rmsnorm(x, g):
    return (x * lax.rsqrt(jnp.mean(x * x, axis=-1, keepdims=True) + EPS)) * g


def _sigmoid(x):
    return 0.5 * jnp.tanh(0.5 * x) + 0.5


def _silu(x):
    h = 0.5 * x
    return h * jnp.tanh(h) + h


def _log_sigmoid(x):
    return jnp.minimum(x, 0.0) - jnp.log1p(jnp.exp(-jnp.abs(x)))


def _softplus(x):
    return jnp.maximum(x, 0.0) + jnp.log1p(jnp.exp(-jnp.abs(x)))


def _dot(a, b):
    return jnp.dot(a, b, preferred_element_type=F32)


def _dot_nt(a, b, precision=None):
    return lax.dot_general(a, b, (((1,), (1,)), ((), ())), precision=precision,
                           preferred_element_type=F32)


def _split_bf16(x):
    hi = x.astype(BF16)
    return hi, (x - hi.astype(F32)).astype(BF16)


def _rows_from_tiles(ref, n_rows, d):
    return jnp.concatenate(
        [ref[pl.ds(j, n_rows, stride=V7X_SUBLANES), :] for j in range(d // V7X_LANES)], axis=1)


def _rows_to_tiles(ref, x):
    n_rows, d = x.shape
    for j in range(d // V7X_LANES):
        ref[pl.ds(j, n_rows, stride=V7X_SUBLANES), :] = x[:, j * V7X_LANES:(j + 1) * V7X_LANES]


def _inproj_kernel(*refs, with_moe, t):
    if with_moe:
        x_ref, o0_ref, o1_ref, g_ref, w_ref, wv_ref, wg_ref, xres_ref, z_ref, vt_ref, zg_ref, xn_s = refs
    else:
        x_ref, g_ref, w_ref, wv_ref, wg_ref, z_ref, vt_ref, zg_ref, xn_s = refs
    tm, d = x_ref.shape
    j = pl.program_id(1)

    @pl.when(j == 0)
    def _():
        x = x_ref[...]
        if with_moe:
            x = x + _rows_from_tiles(o0_ref, tm, d) + _rows_from_tiles(o1_ref, tm, d)
            xres_ref[...] = x
        xn_s[...] = _rmsnorm(x, g_ref[...]).astype(BF16)
        zg_ref[...] = _dot(xn_s[...], wg_ref[...])
        v = _dot(xn_s[...], wv_ref[...])
        tp = -(-t // V7X_LANES) * V7X_LANES
        for q in range(tm // t):
            vq = v[q * t:(q + 1) * t, :]
            if tp != t:
                vq = jnp.concatenate([vq, jnp.zeros((tp - t, d), F32)], axis=0)
            vt_ref[q] = vq.T[:, :t]

    @pl.when(j > 0)
    def _():
        z_ref[...] = _dot(xn_s[...], w_ref[...])


def _inproj(x, moe_out, p, *, tm, tn, t):
    n, d = x.shape
    zc = p["w_z"].shape[1]
    nrt = n // tm
    rt = V7X_SUBLANES
    parts = [x]
    part_specs = [pl.BlockSpec((tm, d), lambda i, j: (i, 0))]
    if moe_out is not None:
        parts += [moe_out, moe_out]
        part_specs += [pl.BlockSpec((tm * rt, V7X_LANES), lambda i, j: (i, 0)),
                       pl.BlockSpec((tm * rt, V7X_LANES), lambda i, j: (i + nrt, 0))]
    out_shape = [jax.ShapeDtypeStruct((n, zc), F32),
                 jax.ShapeDtypeStruct((n // t, d, t), F32),
                 jax.ShapeDtypeStruct((n, GATE_LANES), F32)]
    zcol = lambda j: jnp.maximum(j - 1, 0)
    out_specs = [pl.BlockSpec((tm, tn), lambda i, j: (i, zcol(j))),
                 pl.BlockSpec((tm // t, d, t), lambda i, j: (i, 0, 0)),
                 pl.BlockSpec((tm, GATE_LANES), lambda i, j: (i, 0))]
    if moe_out is not None:
        out_shape = [jax.ShapeDtypeStruct((n, d), F32)] + out_shape
        out_specs = [pl.BlockSpec((tm, d), lambda i, j: (i, 0))] + out_specs
    res = pl.pallas_call(
        functools.partial(_inproj_kernel, with_moe=moe_out is not None, t=t),
        grid=(nrt, zc // tn + 1),
        in_specs=part_specs + [pl.BlockSpec((1, d), lambda i, j: (0, 0)),
                               pl.BlockSpec((d, tn), lambda i, j: (0, zcol(j))),
                               pl.BlockSpec((d, d), lambda i, j: (0, 0)),
                               pl.BlockSpec((d, GATE_LANES), lambda i, j: (0, 0))],
        out_specs=out_specs,
        out_shape=out_shape,
        scratch_shapes=[pltpu.VMEM((tm, d), BF16)],
        compiler_params=pltpu.CompilerParams(
            dimension_semantics=("parallel", "arbitrary"),
            vmem_limit_bytes=VMEM_LIMIT_BYTES),
        name="inproj",
    )(*parts, p["g_mix"], p["w_z"], p["w_v"], p["w_gates"])
    if moe_out is not None:
        return res
    return [x] + list(res)


def _mixer_kernel(zc_ref, zo_ref, zga_ref, zgb_ref, vt_ref, zg_ref, x_ref,
                  conv0_ref, h0_ref, c0_ref, n0_ref, m0_ref,
                  cw_ref, cb_ref, wri_ref, br_ref, bi_ref, lam_ref, gbias_ref, mgn_ref,
                  wpa_ref, wpb_ref, wout_ref, gffn_ref, wrh_ref, wrl_ref, brt_ref,
                  xnew_ref, xn2_ref, rid_ref, rw_ref,
                  convs_ref, hlast_ref, cout_ref, nout_ref, mout_ref,
                  zbuf, xl_s, q_s, k_s, a_s, u_s, h_s, hm_s, c_s, n_s, m_s, hc_s,
                  *, chunk, heads):
    t = x_ref.shape[0]
    d = x_ref.shape[1]
    hd = d // heads
    step = pl.program_id(1)
    last_step = pl.num_programs(1) - 1
    tail = V7X_SUBLANES
    first_tap = tail - (CONV_W - 1)

    @pl.when(step == 0)
    def _init():
        zbuf[0:tail, :] = conv0_ref[0]
        hc_s[...] = h0_ref[0]
        c_s[...] = c0_ref[0]
        n_s[...] = n0_ref[0]
        m_s[...] = m0_ref[0]

    zbuf[tail:tail + t, :] = zc_ref[...]

    lead_rows = lax.broadcasted_iota(jnp.int32, (tail, d), 0)

    def conv_cols(lo, hi):
        cur = zbuf[tail:tail + t, lo:hi]
        prev = zbuf[0:tail, lo:hi]
        acc = None
        for tap in range(CONV_W):
            s = CONV_W - 1 - tap
            if s == 0:
                term = cur
            else:
                rolled = pltpu.roll(cur, s, 0)
                head = jnp.where(lead_rows < s, pltpu.roll(prev, s, 0), rolled[0:tail])
                term = jnp.concatenate([head, rolled[tail:]], axis=0) if t > tail else head
            term = term * cw_ref[tap:tap + 1, lo:hi]
            acc = term if acc is None else acc + term
        return acc + cb_ref[:, lo:hi]

    q_s[...] = _silu(conv_cols(d, 2 * d)).astype(BF16)
    k_s[...] = (_silu(conv_cols(2 * d, 3 * d)) * (hd ** -0.5)).astype(BF16)

    tc = chunk
    gates = zg_ref[...] + gbias_ref[...]
    lf = _log_sigmoid(gates)
    tri_r = lax.broadcasted_iota(jnp.int32, (tc, tc), 0)
    tri_c = lax.broadcasted_iota(jnp.int32, (tc, tc), 1)
    causal = tri_c <= tri_r
    tri = causal.astype(F32)
    pick = (lax.broadcasted_iota(jnp.int32, (V7X_SUBLANES, GATE_LANES), 0) ==
            lax.broadcasted_iota(jnp.int32, (V7X_SUBLANES, GATE_LANES), 1)).astype(F32)
    for j in range(t // tc):
        r0, r1 = j * tc, (j + 1) * tc
        bcum = jnp.dot(tri, lf[r0:r1], precision=lax.Precision.HIGHEST,
                       preferred_element_type=F32)
        b_al = pltpu.roll(bcum, GATE_LANES - heads, 1)
        g_al = gates[r0:r1] - b_al
        g_rows = _dot_nt(pick, g_al, precision=lax.Precision.HIGHEST)
        for h in range(heads):
            c0, c1 = h * hd, (h + 1) * hd
            qh = q_s[r0:r1, c0:c1]
            kh = k_s[r0:r1, c0:c1]
            vth = vt_ref[0, c0:c1, r0:r1].astype(BF16)
            bcol = b_al[:, h:h + 1]
            gcol = g_al[:, h:h + 1]
            m_prev = m_s[h:h + 1, 0:1]
            dlog = jnp.where(causal, bcol + g_rows[h:h + 1, :], -jnp.inf)
            m_inter = bcol + m_prev
            m_t = jnp.maximum(m_inter, jnp.max(dlog, axis=-1, keepdims=True))
            s = _dot_nt(qh, kh) * jnp.exp(dlog - m_t)
            inter = jnp.exp(m_inter - m_t)
            qc_ = _dot_nt(qh, c_s[h].astype(BF16))
            num = _dot_nt(s.astype(BF16), vth) + inter * qc_
            qn = jnp.sum(qh.astype(F32) * n_s[h:h + 1, :], axis=-1, keepdims=True)
            den = jnp.sum(s, axis=-1, keepdims=True) + inter * qn
            hm_s[r0:r1, c0:c1] = num / jnp.maximum(jnp.abs(den), jnp.exp(-m_t))
            b_end = bcol[tc - 1:tc, :]
            m_new = m_t[tc - 1:tc, :]
            wk = jnp.exp(b_end + gcol - m_new)
            decay = jnp.exp(b_end + m_prev - m_new)
            kw = wk * kh.astype(F32)
            c_s[h] = decay * c_s[h] + _dot(vth, kw.astype(BF16))
            n_s[h:h + 1, :] = decay * n_s[h:h + 1, :] + jnp.sum(kw, axis=0, keepdims=True)
            m_s[h:h + 1, :] = jnp.broadcast_to(m_new, (1, V7X_LANES))

    xl_s[...] = conv_cols(0, d)
    zbuf[0:tail, :] = zbuf[t:t + tail, :]
    gw = V7X_MXU_DIM
    sp = _softplus(-lam_ref[...])
    for g in range(d // gw):
        lo, hi = g * gw, (g + 1) * gw
        xg = xl_s[:, lo:hi]
        ri = _dot(xg.astype(BF16), wri_ref[g])
        r = _sigmoid(ri[:, :gw] + br_ref[:, lo:hi])
        ig = _sigmoid(ri[:, gw:] + bi_ref[:, lo:hi])
        log_a = (-LRU_C * r) * sp[:, lo:hi]
        a = jnp.exp(log_a)
        sq = -jnp.tanh(log_a) * (a * a + 1.0)
        mult = jnp.where(sq > 0.0, sq * lax.rsqrt(sq), 0.0)
        a_s[:, lo:hi] = a
        u_s[:, lo:hi] = mult * (ig * xg)

    rows = lax.broadcasted_iota(jnp.int32, (V7X_SUBLANES, d), 0)
    shifts = (1, 2, 4)
    masks = [rows >= s for s in shifts]
    hc = hc_s[...]
    for gi in range(t // V7X_SUBLANES):
        r0 = gi * V7X_SUBLANES
        a = a_s[r0:r0 + V7X_SUBLANES, :]
        u = u_s[r0:r0 + V7X_SUBLANES, :]
        for s, mk in zip(shifts, masks):
            a_sh = jnp.where(mk, pltpu.roll(a, s, 0), 1.0)
            u_sh = jnp.where(mk, pltpu.roll(u, s, 0), 0.0)
            u = a * u_sh + u
            a = a * a_sh
        hg = u + a * hc
        h_s[r0:r0 + V7X_SUBLANES, :] = hg
        hc = hg[V7X_SUBLANES - 1:V7X_SUBLANES, :]
    hc_s[...] = hc

    for h in range(heads):
        c0, c1 = h * hd, (h + 1) * hd
        seg = hm_s[:, c0:c1]
        seg = seg * lax.rsqrt(jnp.mean(seg * seg, axis=-1, keepdims=True) + EPS)
        hm_s[:, c0:c1] = (seg * mgn_ref[:, c0:c1]) * _sigmoid(zo_ref[:, c0:c1])
    pa = _dot(h_s[...].astype(BF16), wpa_ref[...])
    pb = _dot(hm_s[...].astype(BF16), wpb_ref[...])
    merged = _sigmoid(zga_ref[...]) * pa + _sigmoid(zgb_ref[...]) * pb
    xnew = x_ref[...] + _dot(merged.astype(BF16), wout_ref[...])
    xnew_ref[...] = xnew
    xn2 = _rmsnorm(xnew, gffn_ref[...])
    _rows_to_tiles(xn2_ref, xn2)

    xh, xl = _split_bf16(xn2)
    logit = (_dot_nt(wrh_ref[...], xh) + _dot_nt(wrh_ref[...], xl) + _dot_nt(wrl_ref[...], xh)
             + brt_ref[...])
    ng, ne = N_GROUPS, EXP_PER_GROUP
    gl = logit[0:ng]
    gidx = lax.broadcasted_iota(jnp.int32, (ng, t), 0).astype(F32)
    idx = lax.broadcasted_iota(jnp.int32, (ne, t), 0).astype(F32)
    gmax = jnp.max(gl, axis=0, keepdims=True)
    g_idx = jnp.min(jnp.where(gl == gmax, gidx, float(ng)), axis=0, keepdims=True)
    p_g = 1.0 / jnp.sum(jnp.exp(gl - gmax), axis=0, keepdims=True)
    el = jnp.zeros((ne, t), F32)
    for g in range(ng):
        el = jnp.where(g_idx == float(g), logit[ng + g * ne:ng + (g + 1) * ne], el)
    v1 = jnp.max(el, axis=0, keepdims=True)
    i1 = jnp.min(jnp.where(el == v1, idx, float(ne)), axis=0, keepdims=True)
    rest = jnp.where(idx == i1, -jnp.inf, el)
    v2 = jnp.max(rest, axis=0, keepdims=True)
    i2 = jnp.min(jnp.where(rest == v2, idx, float(ne)), axis=0, keepdims=True)
    e21 = jnp.exp(v2 - v1)
    w1 = (1.0 / (1.0 + e21)) * p_g
    w2 = (e21 / (1.0 + e21)) * p_g
    e1 = g_idx * float(ne) + i1
    e2 = g_idx * float(ne) + i2
    slot = lax.broadcasted_iota(jnp.int32, (V7X_SUBLANES, t), 0)
    rid_ref[0] = jnp.where(slot == 0, e1, jnp.where(slot == 1, e2, 0.0)).astype(jnp.int32)
    rw_ref[0] = jnp.where(slot == 0, w1, jnp.where(slot == 1, w2, 0.0))

    @pl.when(step == last_step)
    def _final_state():
        convs_ref[0] = zbuf[0:tail, :]
        hlast_ref[0] = hc_s[...]
        cout_ref[0] = c_s[...]
        nout_ref[0] = n_s[...]
        mout_ref[0] = m_s[...]


def _mixer(z, vt, zg, x, conv0p, h0, c0, n0, m0p, p, *, batch, seq, t, chunk):
    n, d = x.shape
    heads = n0.shape[1]
    hd = n0.shape[2]
    nt = seq // t
    cc = conv0p.shape[2]
    rt = V7X_SUBLANES
    zblk = lambda width, col: pl.BlockSpec((t, width), lambda b, c: (b * nt + c, col))
    const = lambda shape: pl.BlockSpec(shape, lambda b, c: (0,) * len(shape))
    per_b = lambda shape: pl.BlockSpec((1,) + shape, lambda b, c: (b,) + (0,) * len(shape))
    tok = pl.BlockSpec((t, d), lambda b, c: (b * nt + c, 0))
    in_specs = [
        zblk(cc, 0), zblk(d, cc // d), zblk(d, cc // d + 1), zblk(d, cc // d + 2),
        pl.BlockSpec((1, d, t), lambda b, c: (b * nt + c, 0, 0)),
        zblk(GATE_LANES, 0), tok,
        per_b((rt, cc)), per_b((1, d)), per_b((heads, hd, hd)), per_b((heads, hd)),
        per_b((rt, V7X_LANES)),
        const((CONV_W, cc)), const((1, cc)), const(p["wri"].shape), const((1, d)), const((1, d)),
        const((1, d)), const((1, GATE_LANES)), const((1, d)),
        const((d, d)), const((d, d)), const((d, d)), const((1, d)),
        const((ROUTE_ROWS, d)), const((ROUTE_ROWS, d)), const((ROUTE_ROWS, 1)),
    ]
    route_blk = pl.BlockSpec((1, rt, t), lambda b, c: (b * nt + c, 0, 0))
    out_specs = (
        tok, pl.BlockSpec((t * rt, V7X_LANES), lambda b, c: (b * nt + c, 0)), route_blk, route_blk,
        per_b((rt, cc)), per_b((1, d)), per_b((heads, hd, hd)), per_b((heads, hd)),
        per_b((rt, V7X_LANES)),
    )
    out_shape = (
        jax.ShapeDtypeStruct((n, d), F32), jax.ShapeDtypeStruct((n * rt, V7X_LANES), F32),
        jax.ShapeDtypeStruct((batch * nt, rt, t), jnp.int32),
        jax.ShapeDtypeStruct((batch * nt, rt, t), F32),
        jax.ShapeDtypeStruct((batch, rt, cc), F32),
        jax.ShapeDtypeStruct((batch, 1, d), F32),
        jax.ShapeDtypeStruct((batch, heads, hd, hd), F32),
        jax.ShapeDtypeStruct((batch, heads, hd), F32),
        jax.ShapeDtypeStruct((batch, rt, V7X_LANES), F32),
    )
    scratch = [
        pltpu.VMEM((t + rt, cc), F32),
        pltpu.VMEM((t, d), F32),
        pltpu.VMEM((t, d), BF16),
        pltpu.VMEM((t, d), BF16),
        pltpu.VMEM((t, d), F32),
        pltpu.VMEM((t, d), F32),
        pltpu.VMEM((t, d), F32),
        pltpu.VMEM((t, d), F32),
        pltpu.VMEM((heads, hd, hd), F32),
        pltpu.VMEM((heads, hd), F32),
        pltpu.VMEM((rt, V7X_LANES), F32),
        pltpu.VMEM((1, d), F32),
    ]
    return pl.pallas_call(
        functools.partial(_mixer_kernel, chunk=chunk, heads=heads),
        grid=(batch, nt),
        in_specs=in_specs,
        out_specs=out_specs,
        out_shape=out_shape,
        scratch_shapes=scratch,
        compiler_params=pltpu.CompilerParams(
            dimension_semantics=("parallel", "arbitrary"),
            vmem_limit_bytes=VMEM_LIMIT_BYTES),
        name="mixer",
    )(z, z, z, z, vt, zg, x, conv0p, h0, c0, n0, m0p,
      p["conv_w"], p["conv_b"], p["wri"], p["b_r"], p["b_i"], p["lam"], p["gate_bias"], p["m_gn"],
      p["w_pa"], p["w_pb"], p["w_out"], p["g_ffn"], p["wrt_hi"], p["wrt_lo"], p["brt"])


def _expert_kernel(te_ref, nv_ref, tok_ref, tokn_ref, dst_ref, wcol_ref, xn_hbm, wgu_ref, wd_ref, o_hbm,
                   xbuf, ybuf, sem_in, sem_out):
    del te_ref
    j = pl.program_id(0)
    nv = nv_ref[0]
    rt = V7X_SUBLANES
    tm = wcol_ref.shape[0]
    d = wgu_ref.shape[1]
    de = wd_ref.shape[1]
    slot = j % 2

    def row_in(tok, r, sl):
        return pltpu.make_async_copy(xn_hbm.at[pl.ds(pl.multiple_of(tok, rt), rt)],
                                     xbuf.at[sl, pl.ds(r * rt, rt)], sem_in.at[sl])

    def row_out(dst, r, sl):
        return pltpu.make_async_copy(ybuf.at[sl, pl.ds(r * rt, rt)],
                                     o_hbm.at[pl.ds(pl.multiple_of(dst, rt), rt)], sem_out.at[sl])

    def gather(idx_ref, sl):
        for r in range(tm):
            row_in(idx_ref[0, 0, r], r, sl).start()

    def gather_wait(sl):
        for r in range(tm):
            row_in(0, r, sl).wait()

    def scatter(sl):
        for r in range(tm):
            row_out(dst_ref[0, 0, r], r, sl).start()

    def scatter_wait(sl):
        for r in range(tm):
            row_out(0, r, sl).wait()

    @pl.when(j == 0)
    def _zero_padding_rows():
        pad0 = o_hbm.shape[0] - 2 * tm * rt
        ybuf[1] = jnp.zeros(ybuf.shape[1:], F32)
        fills = [pltpu.make_async_copy(ybuf.at[1], o_hbm.at[pl.ds(pad0 + half * tm * rt, tm * rt)],
                                       sem_out.at[1]) for half in range(2)]
        for cp in fills:
            cp.start()
        for cp in fills:
            cp.wait()

    @pl.when(j < nv)
    def _tile():
        @pl.when(j == 0)
        def _():
            gather(tok_ref, slot)

        @pl.when(j + 1 < nv)
        def _():
            gather(tokn_ref, 1 - slot)

        gather_wait(slot)

        x = _rows_from_tiles(xbuf.at[slot], tm, d).astype(BF16)
        gu = _dot(x, wgu_ref[0])
        hid = _silu(gu[:, :de]) * gu[:, de:]
        y = _dot(hid.astype(BF16), wd_ref[0]) * wcol_ref[...]
        _rows_to_tiles(ybuf.at[slot], y)

        scatter(slot)

        @pl.when(j >= 1)
        def _():
            scatter_wait(1 - slot)

        @pl.when(j == nv - 1)
        def _():
            scatter_wait(slot)


def _experts(xn2t, e_flat, w_flat, wgu, wd, *, n_tokens, tm):
    n = n_tokens
    n_exp, de, d = wd.shape
    na = e_flat.shape[0]
    rt = V7X_SUBLANES
    p_rows = na + n_exp * tm
    ntile = p_rows // tm

    cnt = jnp.sum((e_flat[:, None] == jnp.arange(n_exp, dtype=jnp.int32)[None, :]).astype(jnp.int32), axis=0)
    pad = (-cnt) % tm
    pad_key = jnp.where(lax.broadcasted_iota(jnp.int32, (n_exp, tm), 1) < pad[:, None],
                        lax.broadcasted_iota(jnp.int32, (n_exp, tm), 0), n_exp).reshape(n_exp * tm)
    keys = jnp.concatenate([e_flat, pad_key])
    wts = jnp.concatenate([w_flat, jnp.zeros((n_exp * tm,), F32)])
    shift = (p_rows - 1).bit_length()
    assert (n_exp + 1) << shift < 2**31
    pos = lax.iota(jnp.int32, p_rows)
    packed_s, w_s = lax.sort((keys * (1 << shift) + pos, wts), num_keys=1)
    key_s = packed_s >> shift
    src_s = packed_s & ((1 << shift) - 1)
    is_pad = src_s >= na
    tok_s = jnp.where(is_pad, 0, jnp.where(src_s >= n, src_s - n, src_s))
    dst_s = jnp.where(is_pad, na + ((pos // tm) % 2) * tm + pos % tm, src_s)
    tile_key = key_s[::tm]
    n_valid = jnp.sum((tile_key < n_exp).astype(jnp.int32)).reshape(1)
    tile_e = jnp.minimum(tile_key, n_exp - 1)

    idx_blk = lambda fn: pl.BlockSpec((1, 1, tm), fn, memory_space=pltpu.SMEM)
    grid_spec = pltpu.PrefetchScalarGridSpec(
        num_scalar_prefetch=2,
        grid=(ntile,),
        in_specs=[
            idx_blk(lambda j, te, nv: (j, 0, 0)),
            idx_blk(lambda j, te, nv: (jnp.minimum(j + 1, ntile - 1), 0, 0)),
            idx_blk(lambda j, te, nv: (j, 0, 0)),
            pl.BlockSpec((tm, 1), lambda j, te, nv: (j, 0)),
            pl.BlockSpec(memory_space=pl.ANY),
            pl.BlockSpec((1, d, 2 * de), lambda j, te, nv: (te[j], 0, 0)),
            pl.BlockSpec((1, de, d), lambda j, te, nv: (te[j], 0, 0)),
        ],
        out_specs=pl.BlockSpec(memory_space=pl.ANY),
        scratch_shapes=[
            pltpu.VMEM((2, tm * rt, V7X_LANES), F32), pltpu.VMEM((2, tm * rt, V7X_LANES), F32),
            pltpu.SemaphoreType.DMA((2,)), pltpu.SemaphoreType.DMA((2,)),
        ],
    )
    tok3 = (tok_s * rt).reshape(ntile, 1, tm)
    dst_s = dst_s * rt
    return pl.pallas_call(
        _expert_kernel,
        grid_spec=grid_spec,
        out_shape=jax.ShapeDtypeStruct(((na + 2 * tm) * rt, V7X_LANES), F32),
        compiler_params=pltpu.CompilerParams(
            dimension_semantics=("arbitrary",),
            vmem_limit_bytes=VMEM_LIMIT_BYTES),
        name="experts",
    )(tile_e, n_valid, tok3, tok3, dst_s.reshape(ntile, 1, tm), w_s.reshape(p_rows, 1), xn2t, wgu, wd)


def _final_kernel(x_ref, o0_ref, o1_ref, g_ref, y_ref):
    tm, d = x_ref.shape
    x = x_ref[...] + _rows_from_tiles(o0_ref, tm, d) + _rows_from_tiles(o1_ref, tm, d)
    y_ref[...] = _rmsnorm(x, g_ref[...])


def _final(x, moe_out, g, *, tm):
    n, d = x.shape
    nrt = n // tm
    rt = V7X_SUBLANES
    return pl.pallas_call(
        _final_kernel,
        grid=(nrt,),
        in_specs=[pl.BlockSpec((tm, d), lambda i: (i, 0)),
                  pl.BlockSpec((tm * rt, V7X_LANES), lambda i: (i, 0)),
                  pl.BlockSpec((tm * rt, V7X_LANES), lambda i: (i + nrt, 0)),
                  pl.BlockSpec((1, d), lambda i: (0, 0))],
        out_specs=pl.BlockSpec((tm, d), lambda i: (i, 0)),
        out_shape=jax.ShapeDtypeStruct((n, d), F32),
        compiler_params=pltpu.CompilerParams(dimension_semantics=("parallel",)),
        name="final_norm",
    )(x, moe_out, moe_out, g)


def _block_diag_tiles(w):
    nb, bw, _ = w.shape
    per = V7X_MXU_DIM // bw
    w4 = w.reshape(nb // per, per, bw, bw)
    eye = jnp.eye(per, dtype=w.dtype)
    return jnp.einsum("gicd,ij->gicjd", w4, eye).reshape(nb // per, per * bw, per * bw)


def _layer_params(l, g_mix, w_in, b_igate, b_fgate, conv_w, conv_b, lru_wr, lru_br, lru_wi, lru_bi,
                  lru_lambda, m_gn, w_pa, w_pb, w_out, g_ffn, w_group, b_group, w_erouter, b_erouter,
                  w_gate, w_up, w_down):
    d = w_in.shape[1]
    cc = conv_w.shape[2]
    heads = b_igate.shape[1]
    dm = m_gn.shape[1]
    assert dm == d and lru_lambda.shape[1] == d and cc == 3 * d
    w = w_in[l]
    v0, o0, gate0 = cc, cc + dm, cc + 2 * dm
    w_z = jnp.concatenate([w[:, :v0], w[:, o0:gate0], w[:, gate0 + 2 * heads:]], axis=1).astype(BF16)
    w_v = w[:, v0:o0].astype(BF16)
    w_gates = jnp.pad(w[:, gate0:gate0 + 2 * heads], ((0, 0), (0, GATE_LANES - 2 * heads))).astype(BF16)
    gate_bias = jnp.pad(jnp.concatenate([b_igate[l], b_fgate[l]]), (0, GATE_LANES - 2 * heads))[None]
    wri = jnp.concatenate([_block_diag_tiles(lru_wr[l]), _block_diag_tiles(lru_wi[l])], axis=2).astype(BF16)
    ng, _, ne = w_erouter.shape[1:]
    wr = jnp.concatenate([w_group[l], jnp.transpose(w_erouter[l], (1, 0, 2)).reshape(d, ng * ne)], axis=1)
    br = jnp.concatenate([b_group[l], b_erouter[l].reshape(ng * ne)])
    rpad = ROUTE_ROWS - wr.shape[1]
    wrt = jnp.pad(wr, ((0, 0), (0, rpad))).T
    wrt_hi = wrt.astype(BF16)
    wrt_lo = (wrt - wrt_hi.astype(F32)).astype(BF16)
    return dict(
        g_mix=g_mix[l][None], w_z=w_z, w_v=w_v, w_gates=w_gates, gate_bias=gate_bias,
        conv_w=conv_w[l], conv_b=conv_b[l][None], wri=wri,
        b_r=lru_br[l][None], b_i=lru_bi[l][None], lam=lru_lambda[l][None], m_gn=m_gn[l][None],
        w_pa=w_pa[l].astype(BF16), w_pb=w_pb[l].astype(BF16), w_out=w_out[l].astype(BF16),
        g_ffn=g_ffn[l][None], wrt_hi=wrt_hi, wrt_lo=wrt_lo, brt=jnp.pad(br, (0, rpad))[:, None],
        wgu=jnp.concatenate([w_gate[l], w_up[l]], axis=2).astype(BF16), wd=w_down[l].astype(BF16),
    )


def _tile_sizes(n, seq, n_exp):
    t = min(256, seq)
    chunk = min(128, t)
    tm_first = min(1024, n)
    tm = min(512, n)
    tm_exp = 512 if 2 * n >= 8 * n_exp * 512 else 128
    return t, chunk, tm_first, tm, tm_exp


def _trunk(x, conv0, h00, c00, n00, m00, layers, g_final):
    batch, seq, d = x.shape
    n = batch * seq
    depth = len(layers)
    t, chunk, tm_first, tm, tm_exp = _tile_sizes(n, seq, layers[0]["wd"].shape[0])
    xf = x.reshape(n, d)
    moe_out = None
    convs, hs, cs, ns, ms = [], [], [], [], []
    heads = n00.shape[2]
    rt = V7X_SUBLANES
    for l in range(depth):
        p = layers[l]
        if moe_out is None:
            tm_in, col_blocks = tm_first, 6
        else:
            tm_in, col_blocks = tm, 3
        xf, z, vt, zg = _inproj(xf, moe_out, p, tm=tm_in, tn=p["w_z"].shape[1] // col_blocks, t=t)
        conv0p = jnp.pad(conv0[l], ((0, 0), (rt - (CONV_W - 1), 0), (0, 0)))
        m0p = jnp.broadcast_to(
            jnp.pad(m00[l], ((0, 0), (0, rt - heads)))[:, :, None], (batch, rt, V7X_LANES))
        (xf, xn2t, rid, rw, conv_s, h_last, c_new, n_new, m_new) = _mixer(
            z, vt, zg, xf, conv0p, h00[l][:, None, :], c00[l], n00[l], m0p, p,
            batch=batch, seq=seq, t=t, chunk=chunk)
        e_flat = jnp.concatenate([rid[:, 0, :].reshape(n), rid[:, 1, :].reshape(n)])
        w_flat = jnp.concatenate([rw[:, 0, :].reshape(n), rw[:, 1, :].reshape(n)])
        moe_out = _experts(xn2t, e_flat, w_flat, p["wgu"], p["wd"], n_tokens=n, tm=tm_exp)
        convs.append(conv_s[:, rt - (CONV_W - 1):, :])
        hs.append(h_last[:, 0, :])
        cs.append(c_new)
        ns.append(n_new)
        ms.append(m_new[:, :heads, 0])
    y = _final(xf, moe_out, g_final[None], tm=tm).reshape(batch, seq, d)
    return y, jnp.stack(convs), jnp.stack(hs), jnp.stack(cs), jnp.stack(ns), jnp.stack(ms)


def kernel(x_prompt, x_sample, state_conv, state_lru, state_mlstm_C, state_mlstm_n, state_mlstm_m,
           g_mix, w_in, b_igate, b_fgate, conv_w, conv_b, lru_wr, lru_br, lru_wi, lru_bi, lru_lambda,
           m_gn, w_pa, w_pb, w_out, g_ffn, w_group, b_group, w_erouter, b_erouter,
           w_gate, w_up, w_down, g_final):
    depth = w_in.shape[0]
    layers = [
        _layer_params(l, g_mix, w_in, b_igate, b_fgate, conv_w, conv_b, lru_wr, lru_br, lru_wi, lru_bi,
                      lru_lambda, m_gn, w_pa, w_pb, w_out, g_ffn, w_group, b_group, w_erouter,
                      b_erouter, w_gate, w_up, w_down)
        for l in range(depth)]
    dt = x_prompt.dtype
    bp = x_prompt.shape[0]
    zeros_like_state = lambda s: jnp.zeros((depth, bp) + s.shape[2:], dt)
    y_s, s_conv, s_lru, s_c, s_n, s_m = _trunk(
        x_sample, state_conv, state_lru, state_mlstm_C, state_mlstm_n, state_mlstm_m,
        layers, g_final)
    y_p, p_conv, p_lru, p_c, p_n, p_m = _trunk(
        x_prompt, zeros_like_state(state_conv), zeros_like_state(state_lru),
        zeros_like_state(state_mlstm_C), zeros_like_state(state_mlstm_n),
        zeros_like_state(state_mlstm_m), layers, g_final)
    return (y_p, y_s, p_conv, p_lru, p_c, p_n, p_m, s_conv, s_lru, s_c, s_n, s_m)
```

```python
import functools

import jax
import jax.numpy as jnp
from jax import lax
from jax.experimental import pallas as pl
from jax.experimental.pallas import tpu as pltpu

F32 = jnp.float32
BF16 = jnp.bfloat16

EPS = 1e-6
LRU_C = 8.0
CONV_W = 4
N_GROUPS = 4
EXP_PER_GROUP = 4

V7X_VMEM_BYTES = 64 * 2**20
V7X_LANES = 128
V7X_SUBLANES = 8
V7X_MXU_DIM = 256

VMEM_LIMIT_BYTES = V7X_VMEM_BYTES - 8 * 2**20

ROUTE_ROWS = 32
GATE_LANES = V7X_LANES


def _rmsnorm(x, g):
    return (x * lax.rsqrt(jnp.mean(x * x, axis=-1, keepdims=True) + EPS)) * g


def _sigmoid(x):
    return 0.5 * jnp.tanh(0.5 * x) + 0.5


def _silu(x):
    h = 0.5 * x
    return h * jnp.tanh(h) + h


def _log_sigmoid(x):
    return jnp.minimum(x, 0.0) - jnp.log1p(jnp.exp(-jnp.abs(x)))


def _softplus(x):
    return jnp.maximum(x, 0.0) + jnp.log1p(jnp.exp(-jnp.abs(x)))


def _dot(a, b):
    return jnp.dot(a, b, preferred_element_type=F32)


def _dot_nt(a, b, precision=None):
    return lax.dot_general(a, b, (((1,), (1,)), ((), ())), precision=precision,
                           preferred_element_type=F32)


def _split_bf16(x):
    hi = x.astype(BF16)
    return hi, (x - hi.astype(F32)).astype(BF16)


def _rows_from_tiles(ref, n_rows, d):
    return jnp.concatenate(
        [ref[pl.ds(j, n_rows, stride=V7X_SUBLANES), :] for j in range(d // V7X_LANES)], axis=1)


def _rows_to_tiles(ref, x):
    n_rows, d = x.shape
    for j in range(d // V7X_LANES):
        ref[pl.ds(j, n_rows, stride=V7X_SUBLANES), :] = x[:, j * V7X_LANES:(j + 1) * V7X_LANES]


def _inproj_kernel(*refs, with_moe, t):
    if with_moe:
        (x_ref, o0_ref, o1_ref, g_ref, w_ref, wv_ref, wg_ref,
         xres_ref, z_ref, vt_ref, zg_ref, xn_s, v_s) = refs
    else:
        x_ref, g_ref, w_ref, wv_ref, wg_ref, z_ref, vt_ref, zg_ref, xn_s, v_s = refs
    tm, d = x_ref.shape
    j = pl.program_id(1)

    @pl.when(j == 0)
    def _():
        x = x_ref[...]
        if with_moe:
            x = x + _rows_from_tiles(o0_ref, tm, d) + _rows_from_tiles(o1_ref, tm, d)
            xres_ref[...] = x
        xn_s[...] = _rmsnorm(x, g_ref[...]).astype(BF16)
        zg_ref[...] = _dot(xn_s[...], wg_ref[...])
        v_s[...] = _dot(xn_s[...], wv_ref[...])

    def z_block():
        z_ref[...] = _dot(xn_s[...], w_ref[...])

    @pl.when(j == 1)
    def _():
        z_block()
        tp = -(-t // V7X_LANES) * V7X_LANES
        for q in range(tm // t):
            vq = v_s[q * t:(q + 1) * t, :]
            if tp != t:
                vq = jnp.concatenate([vq, jnp.zeros((tp - t, d), F32)], axis=0)
            vt_ref[q] = vq.T[:, :t]

    @pl.when(j > 1)
    def _():
        z_block()


def _inproj(x, moe_out, p, *, tm, tn, t):
    n, d = x.shape
    zc = p["w_z"].shape[1]
    nrt = n // tm
    rt = V7X_SUBLANES
    parts = [x]
    part_specs = [pl.BlockSpec((tm, d), lambda i, j: (i, 0))]
    if moe_out is not None:
        parts += [moe_out, moe_out]
        part_specs += [pl.BlockSpec((tm * rt, V7X_LANES), lambda i, j: (i, 0)),
                       pl.BlockSpec((tm * rt, V7X_LANES), lambda i, j: (i + nrt, 0))]
    out_shape = [jax.ShapeDtypeStruct((n, zc), F32),
                 jax.ShapeDtypeStruct((n // t, d, t), F32),
                 jax.ShapeDtypeStruct((n, GATE_LANES), F32)]
    zcol = lambda j: jnp.maximum(j - 1, 0)
    out_specs = [pl.BlockSpec((tm, tn), lambda i, j: (i, zcol(j))),
                 pl.BlockSpec((tm // t, d, t), lambda i, j: (i, 0, 0)),
                 pl.BlockSpec((tm, GATE_LANES), lambda i, j: (i, 0))]
    if moe_out is not None:
        out_shape = [jax.ShapeDtypeStruct((n, d), F32)] + out_shape
        out_specs = [pl.BlockSpec((tm, d), lambda i, j: (i, 0))] + out_specs
    res = pl.pallas_call(
        functools.partial(_inproj_kernel, with_moe=moe_out is not None, t=t),
        grid=(nrt, zc // tn + 1),
        in_specs=part_specs + [pl.BlockSpec((1, d), lambda i, j: (0, 0)),
                               pl.BlockSpec((d, tn), lambda i, j: (0, zcol(j))),
                               pl.BlockSpec((d, d), lambda i, j: (0, 0)),
                               pl.BlockSpec((d, GATE_LANES), lambda i, j: (0, 0))],
        out_specs=out_specs,
        out_shape=out_shape,
        scratch_shapes=[pltpu.VMEM((tm, d), BF16), pltpu.VMEM((tm, d), F32)],
        compiler_params=pltpu.CompilerParams(
            dimension_semantics=("parallel", "arbitrary"),
            vmem_limit_bytes=VMEM_LIMIT_BYTES),
        name="inproj",
    )(*parts, p["g_mix"], p["w_z"], p["w_v"], p["w_gates"])
    if moe_out is not None:
        return res
    return [x] + list(res)


def _mixer_kernel(zc_ref, zo_ref, zga_ref, zgb_ref, vt_ref, zg_ref, x_ref,
                  conv0_ref, h0_ref, c0_ref, n0_ref, m0_ref,
                  cw_ref, cb_ref, wri_ref, br_ref, bi_ref, lam_ref, gbias_ref, mgn_ref,
                  wpa_ref, wpb_ref, wout_ref, gffn_ref, wrh_ref, wrl_ref, brt_ref,
                  xnew_ref, xn2_ref, rid_ref, rw_ref,
                  convs_ref, hlast_ref, cout_ref, nout_ref, mout_ref,
                  zbuf, xl_s, q_s, k_s, a_s, u_s, h_s, hm_s, c_s, n_s, m_s, hc_s,
                  *, chunk, heads):
    t = x_ref.shape[0]
    d = x_ref.shape[1]
    hd = d // heads
    step = pl.program_id(1)
    last_step = pl.num_programs(1) - 1
    tail = V7X_SUBLANES
    first_tap = tail - (CONV_W - 1)

    @pl.when(step == 0)
    def _init():
        zbuf[0:tail, :] = conv0_ref[0]
        hc_s[...] = h0_ref[0]
        c_s[...] = c0_ref[0]
        n_s[...] = n0_ref[0]
        m_s[...] = m0_ref[0]

    zbuf[tail:tail + t, :] = zc_ref[...]

    lead_rows = lax.broadcasted_iota(jnp.int32, (tail, d), 0)

    def conv_cols(lo, hi):
        cur = zbuf[tail:tail + t, lo:hi]
        prev = zbuf[0:tail, lo:hi]
        acc = None
        for tap in range(CONV_W):
            s = CONV_W - 1 - tap
            if s == 0:
                term = cur
            else:
                rolled = pltpu.roll(cur, s, 0)
                head = jnp.where(lead_rows < s, pltpu.roll(prev, s, 0), rolled[0:tail])
                term = jnp.concatenate([head, rolled[tail:]], axis=0) if t > tail else head
            term = term * cw_ref[tap:tap + 1, lo:hi]
            acc = term if acc is None else acc + term
        return acc + cb_ref[:, lo:hi]

    q_s[...] = _silu(conv_cols(d, 2 * d)).astype(BF16)
    k_s[...] = (_silu(conv_cols(2 * d, 3 * d)) * (hd ** -0.5)).astype(BF16)

    tc = chunk
    gates = zg_ref[...] + gbias_ref[...]
    lf = _log_sigmoid(gates)
    tri_r = lax.broadcasted_iota(jnp.int32, (tc, tc), 0)
    tri_c = lax.broadcasted_iota(jnp.int32, (tc, tc), 1)
    causal = tri_c <= tri_r
    tri = causal.astype(F32)
    pick = (lax.broadcasted_iota(jnp.int32, (V7X_SUBLANES, GATE_LANES), 0) ==
            lax.broadcasted_iota(jnp.int32, (V7X_SUBLANES, GATE_LANES), 1)).astype(F32)
    for j in range(t // tc):
        r0, r1 = j * tc, (j + 1) * tc
        bcum = jnp.dot(tri, lf[r0:r1], precision=lax.Precision.HIGHEST,
                       preferred_element_type=F32)
        b_al = pltpu.roll(bcum, GATE_LANES - heads, 1)
        g_al = gates[r0:r1] - b_al
        g_rows = _dot_nt(pick, g_al, precision=lax.Precision.HIGHEST)
        for h in range(heads):
            c0, c1 = h * hd, (h + 1) * hd
            qh = q_s[r0:r1, c0:c1]
            kh = k_s[r0:r1, c0:c1]
            vth = vt_ref[0, c0:c1, r0:r1].astype(BF16)
            bcol = b_al[:, h:h + 1]
            gcol = g_al[:, h:h + 1]
            m_prev = m_s[h:h + 1, 0:1]
            dlog = jnp.where(causal, bcol + g_rows[h:h + 1, :], -jnp.inf)
            m_inter = bcol + m_prev
            m_t = jnp.maximum(m_inter, jnp.max(dlog, axis=-1, keepdims=True))
            s = _dot_nt(qh, kh) * jnp.exp(dlog - m_t)
            inter = jnp.exp(m_inter - m_t)
            qc_ = _dot_nt(qh, c_s[h].astype(BF16))
            num = _dot_nt(s.astype(BF16), vth) + inter * qc_
            qn = jnp.sum(qh.astype(F32) * n_s[h:h + 1, :], axis=-1, keepdims=True)
            den = jnp.sum(s, axis=-1, keepdims=True) + inter * qn
            hm_s[r0:r1, c0:c1] = num / jnp.maximum(jnp.abs(den), jnp.exp(-m_t))
            b_end = bcol[tc - 1:tc, :]
            m_new = m_t[tc - 1:tc, :]
            wk = jnp.exp(b_end + gcol - m_new)
            decay = jnp.exp(b_end + m_prev - m_new)
            kw = wk * kh.astype(F32)
            c_s[h] = decay * c_s[h] + _dot(vth, kw.astype(BF16))
            n_s[h:h + 1, :] = decay * n_s[h:h + 1, :] + jnp.sum(kw, axis=0, keepdims=True)
            m_s[h:h + 1, :] = jnp.broadcast_to(m_new, (1, V7X_LANES))

    xl_s[...] = conv_cols(0, d)
    zbuf[0:tail, :] = zbuf[t:t + tail, :]
    gw = V7X_MXU_DIM
    sp = _softplus(-lam_ref[...])
    for g in range(d // gw):
        lo, hi = g * gw, (g + 1) * gw
        xg = xl_s[:, lo:hi]
        ri = _dot(xg.astype(BF16), wri_ref[g])
        r = _sigmoid(ri[:, :gw] + br_ref[:, lo:hi])
        ig = _sigmoid(ri[:, gw:] + bi_ref[:, lo:hi])
        log_a = (-LRU_C * r) * sp[:, lo:hi]
        a = jnp.exp(log_a)
        sq = -jnp.tanh(log_a) * (a * a + 1.0)
        mult = jnp.where(sq > 0.0, sq * lax.rsqrt(sq), 0.0)
        a_s[:, lo:hi] = a
        u_s[:, lo:hi] = mult * (ig * xg)

    rows = lax.broadcasted_iota(jnp.int32, (V7X_SUBLANES, d), 0)
    shifts = (1, 2, 4)
    masks = [rows >= s for s in shifts]
    hc = hc_s[...]
    for gi in range(t // V7X_SUBLANES):
        r0 = gi * V7X_SUBLANES
        a = a_s[r0:r0 + V7X_SUBLANES, :]
        u = u_s[r0:r0 + V7X_SUBLANES, :]
        for s, mk in zip(shifts, masks):
            a_sh = jnp.where(mk, pltpu.roll(a, s, 0), 1.0)
            u_sh = jnp.where(mk, pltpu.roll(u, s, 0), 0.0)
            u = a * u_sh + u
            a = a * a_sh
        hg = u + a * hc
        h_s[r0:r0 + V7X_SUBLANES, :] = hg
        hc = hg[V7X_SUBLANES - 1:V7X_SUBLANES, :]
    hc_s[...] = hc

    for h in range(heads):
        c0, c1 = h * hd, (h + 1) * hd
        seg = hm_s[:, c0:c1]
        seg = seg * lax.rsqrt(jnp.mean(seg * seg, axis=-1, keepdims=True) + EPS)
        hm_s[:, c0:c1] = (seg * mgn_ref[:, c0:c1]) * _sigmoid(zo_ref[:, c0:c1])
    pa = _dot(h_s[...].astype(BF16), wpa_ref[...])
    pb = _dot(hm_s[...].astype(BF16), wpb_ref[...])
    merged = _sigmoid(zga_ref[...]) * pa + _sigmoid(zgb_ref[...]) * pb
    xnew = x_ref[...] + _dot(merged.astype(BF16), wout_ref[...])
    xnew_ref[...] = xnew
    xn2 = _rmsnorm(xnew, gffn_ref[...])
    _rows_to_tiles(xn2_ref, xn2)

    xh, xl = _split_bf16(xn2)
    logit = (_dot_nt(wrh_ref[...], xh) + _dot_nt(wrh_ref[...], xl) + _dot_nt(wrl_ref[...], xh)
             + brt_ref[...])
    ng, ne = N_GROUPS, EXP_PER_GROUP
    gl = logit[0:ng]
    gidx = lax.broadcasted_iota(jnp.int32, (ng, t), 0).astype(F32)
    idx = lax.broadcasted_iota(jnp.int32, (ne, t), 0).astype(F32)
    gmax = jnp.max(gl, axis=0, keepdims=True)
    g_idx = jnp.min(jnp.where(gl == gmax, gidx, float(ng)), axis=0, keepdims=True)
    p_g = 1.0 / jnp.sum(jnp.exp(gl - gmax), axis=0, keepdims=True)
    el = jnp.zeros((ne, t), F32)
    for g in range(ng):
        el = jnp.where(g_idx == float(g), logit[ng + g * ne:ng + (g + 1) * ne], el)
    v1 = jnp.max(el, axis=0, keepdims=True)
    i1 = jnp.min(jnp.where(el == v1, idx, float(ne)), axis=0, keepdims=True)
    rest = jnp.where(idx == i1, -jnp.inf, el)
    v2 = jnp.max(rest, axis=0, keepdims=True)
    i2 = jnp.min(jnp.where(rest == v2, idx, float(ne)), axis=0, keepdims=True)
    e21 = jnp.exp(v2 - v1)
    w1 = (1.0 / (1.0 + e21)) * p_g
    w2 = (e21 / (1.0 + e21)) * p_g
    e1 = g_idx * float(ne) + i1
    e2 = g_idx * float(ne) + i2
    slot = lax.broadcasted_iota(jnp.int32, (V7X_SUBLANES, t), 0)
    rid_ref[0] = jnp.where(slot == 0, e1, jnp.where(slot == 1, e2, 0.0)).astype(jnp.int32)
    rw_ref[0] = jnp.where(slot == 0, w1, jnp.where(slot == 1, w2, 0.0))

    @pl.when(step == last_step)
    def _final_state():
        convs_ref[0] = zbuf[0:tail, :]
        hlast_ref[0] = hc_s[...]
        cout_ref[0] = c_s[...]
        nout_ref[0] = n_s[...]
        mout_ref[0] = m_s[...]


def _mixer(z, vt, zg, x, conv0p, h0, c0, n0, m0p, p, *, batch, seq, t, chunk):
    n, d = x.shape
    heads = n0.shape[1]
    hd = n0.shape[2]
    nt = seq // t
    cc = conv0p.shape[2]
    rt = V7X_SUBLANES
    zblk = lambda width, col: pl.BlockSpec((t, width), lambda b, c: (b * nt + c, col))
    const = lambda shape: pl.BlockSpec(shape, lambda b, c: (0,) * len(shape))
    per_b = lambda shape: pl.BlockSpec((1,) + shape, lambda b, c: (b,) + (0,) * len(shape))
    tok = pl.BlockSpec((t, d), lambda b, c: (b * nt + c, 0))
    in_specs = [
        zblk(cc, 0), zblk(d, cc // d), zblk(d, cc // d + 1), zblk(d, cc // d + 2),
        pl.BlockSpec((1, d, t), lambda b, c: (b * nt + c, 0, 0)),
        zblk(GATE_LANES, 0), tok,
        per_b((rt, cc)), per_b((1, d)), per_b((heads, hd, hd)), per_b((heads, hd)),
        per_b((rt, V7X_LANES)),
        const((CONV_W, cc)), const((1, cc)), const(p["wri"].shape), const((1, d)), const((1, d)),
        const((1, d)), const((1, GATE_LANES)), const((1, d)),
        const((d, d)), const((d, d)), const((d, d)), const((1, d)),
        const((ROUTE_ROWS, d)), const((ROUTE_ROWS, d)), const((ROUTE_ROWS, 1)),
    ]
    route_blk = pl.BlockSpec((1, rt, t), lambda b, c: (b * nt + c, 0, 0))
    out_specs = (
        tok, pl.BlockSpec((t * rt, V7X_LANES), lambda b, c: (b * nt + c, 0)), route_blk, route_blk,
        per_b((rt, cc)), per_b((1, d)), per_b((heads, hd, hd)), per_b((heads, hd)),
        per_b((rt, V7X_LANES)),
    )
    out_shape = (
        jax.ShapeDtypeStruct((n, d), F32), jax.ShapeDtypeStruct((n * rt, V7X_LANES), F32),
        jax.ShapeDtypeStruct((batch * nt, rt, t), jnp.int32),
        jax.ShapeDtypeStruct((batch * nt, rt, t), F32),
        jax.ShapeDtypeStruct((batch, rt, cc), F32),
        jax.ShapeDtypeStruct((batch, 1, d), F32),
        jax.ShapeDtypeStruct((batch, heads, hd, hd), F32),
        jax.ShapeDtypeStruct((batch, heads, hd), F32),
        jax.ShapeDtypeStruct((batch, rt, V7X_LANES), F32),
    )
    scratch = [
        pltpu.VMEM((t + rt, cc), F32),
        pltpu.VMEM((t, d), F32),
        pltpu.VMEM((t, d), BF16),
        pltpu.VMEM((t, d), BF16),
        pltpu.VMEM((t, d), F32),
        pltpu.VMEM((t, d), F32),
        pltpu.VMEM((t, d), F32),
        pltpu.VMEM((t, d), F32),
        pltpu.VMEM((heads, hd, hd), F32),
        pltpu.VMEM((heads, hd), F32),
        pltpu.VMEM((rt, V7X_LANES), F32),
        pltpu.VMEM((1, d), F32),
    ]
    return pl.pallas_call(
        functools.partial(_mixer_kernel, chunk=chunk, heads=heads),
        grid=(batch, nt),
        in_specs=in_specs,
        out_specs=out_specs,
        out_shape=out_shape,
        scratch_shapes=scratch,
        compiler_params=pltpu.CompilerParams(
            dimension_semantics=("parallel", "arbitrary"),
            vmem_limit_bytes=VMEM_LIMIT_BYTES),
        name="mixer",
    )(z, z, z, z, vt, zg, x, conv0p, h0, c0, n0, m0p,
      p["conv_w"], p["conv_b"], p["wri"], p["b_r"], p["b_i"], p["lam"], p["gate_bias"], p["m_gn"],
      p["w_pa"], p["w_pb"], p["w_out"], p["g_ffn"], p["wrt_hi"], p["wrt_lo"], p["brt"])


def _expert_kernel(te_ref, nv_ref, tok_ref, tokn_ref, dst_ref, wcol_ref, xn_hbm, wgu_ref, wd_ref, o_hbm,
                   xbuf, ybuf, sem_in, sem_out):
    del te_ref
    j = pl.program_id(0)
    nv = nv_ref[0]
    rt = V7X_SUBLANES
    tm = wcol_ref.shape[0]
    d = wgu_ref.shape[1]
    de = wd_ref.shape[1]
    slot = j % 2

    def row_in(tok, r, sl):
        return pltpu.make_async_copy(xn_hbm.at[pl.ds(pl.multiple_of(tok, rt), rt)],
                                     xbuf.at[sl, pl.ds(r * rt, rt)], sem_in.at[sl])

    def row_out(dst, r, sl):
        return pltpu.make_async_copy(ybuf.at[sl, pl.ds(r * rt, rt)],
                                     o_hbm.at[pl.ds(pl.multiple_of(dst, rt), rt)], sem_out.at[sl])

    def gather(idx_ref, sl):
        for r in range(tm):
            row_in(idx_ref[0, 0, r], r, sl).start()

    def gather_wait(sl):
        for r in range(tm):
            row_in(0, r, sl).wait()

    def scatter(sl):
        for r in range(tm):
            row_out(dst_ref[0, 0, r], r, sl).start()

    def scatter_wait(sl):
        for r in range(tm):
            row_out(0, r, sl).wait()

    @pl.when(j == 0)
    def _zero_padding_rows():
        pad0 = o_hbm.shape[0] - 2 * tm * rt
        ybuf[1] = jnp.zeros(ybuf.shape[1:], F32)
        fills = [pltpu.make_async_copy(ybuf.at[1], o_hbm.at[pl.ds(pad0 + half * tm * rt, tm * rt)],
                                       sem_out.at[1]) for half in range(2)]
        for cp in fills:
            cp.start()
        for cp in fills:
            cp.wait()

    @pl.when(j < nv)
    def _tile():
        @pl.when(j == 0)
        def _():
            gather(tok_ref, slot)

        @pl.when(j + 1 < nv)
        def _():
            gather(tokn_ref, 1 - slot)

        gather_wait(slot)

        x = _rows_from_tiles(xbuf.at[slot], tm, d).astype(BF16)
        gu = _dot(x, wgu_ref[0])
        hid = _silu(gu[:, :de]) * gu[:, de:]
        y = _dot(hid.astype(BF16), wd_ref[0]) * wcol_ref[...]
        _rows_to_tiles(ybuf.at[slot], y)

        scatter(slot)

        @pl.when(j >= 1)
        def _():
            scatter_wait(1 - slot)

        @pl.when(j == nv - 1)
        def _():
            scatter_wait(slot)


def _experts(xn2t, e_flat, w_flat, wgu, wd, *, n_tokens, tm):
    n = n_tokens
    n_exp, de, d = wd.shape
    na = e_flat.shape[0]
    rt = V7X_SUBLANES
    p_rows = na + n_exp * tm
    ntile = p_rows // tm

    cnt = jnp.sum((e_flat[:, None] == jnp.arange(n_exp, dtype=jnp.int32)[None, :]).astype(jnp.int32), axis=0)
    pad = (-cnt) % tm
    pad_key = jnp.where(lax.broadcasted_iota(jnp.int32, (n_exp, tm), 1) < pad[:, None],
                        lax.broadcasted_iota(jnp.int32, (n_exp, tm), 0), n_exp).reshape(n_exp * tm)
    keys = jnp.concatenate([e_flat, pad_key])
    wts = jnp.concatenate([w_flat, jnp.zeros((n_exp * tm,), F32)])
    shift = (p_rows - 1).bit_length()
    assert (n_exp + 1) << shift < 2**31
    pos = lax.iota(jnp.int32, p_rows)
    packed_s, w_s = lax.sort((keys * (1 << shift) + pos, wts), num_keys=1)
    key_s = packed_s >> shift
    src_s = packed_s & ((1 << shift) - 1)
    is_pad = src_s >= na
    tok_s = jnp.where(is_pad, 0, jnp.where(src_s >= n, src_s - n, src_s))
    dst_s = jnp.where(is_pad, na + ((pos // tm) % 2) * tm + pos % tm, src_s)
    tile_key = key_s[::tm]
    n_valid = jnp.sum((tile_key < n_exp).astype(jnp.int32)).reshape(1)
    tile_e = jnp.minimum(tile_key, n_exp - 1)

    idx_blk = lambda fn: pl.BlockSpec((1, 1, tm), fn, memory_space=pltpu.SMEM)
    grid_spec = pltpu.PrefetchScalarGridSpec(
        num_scalar_prefetch=2,
        grid=(ntile,),
        in_specs=[
            idx_blk(lambda j, te, nv: (j, 0, 0)),
            idx_blk(lambda j, te, nv: (jnp.minimum(j + 1, ntile - 1), 0, 0)),
            idx_blk(lambda j, te, nv: (j, 0, 0)),
            pl.BlockSpec((tm, 1), lambda j, te, nv: (j, 0)),
            pl.BlockSpec(memory_space=pl.ANY),
            pl.BlockSpec((1, d, 2 * de), lambda j, te, nv: (te[j], 0, 0)),
            pl.BlockSpec((1, de, d), lambda j, te, nv: (te[j], 0, 0)),
        ],
        out_specs=pl.BlockSpec(memory_space=pl.ANY),
        scratch_shapes=[
            pltpu.VMEM((2, tm * rt, V7X_LANES), F32), pltpu.VMEM((2, tm * rt, V7X_LANES), F32),
            pltpu.SemaphoreType.DMA((2,)), pltpu.SemaphoreType.DMA((2,)),
        ],
    )
    tok3 = (tok_s * rt).reshape(ntile, 1, tm)
    dst_s = dst_s * rt
    return pl.pallas_call(
        _expert_kernel,
        grid_spec=grid_spec,
        out_shape=jax.ShapeDtypeStruct(((na + 2 * tm) * rt, V7X_LANES), F32),
        compiler_params=pltpu.CompilerParams(
            dimension_semantics=("arbitrary",),
            vmem_limit_bytes=VMEM_LIMIT_BYTES),
        name="experts",
    )(tile_e, n_valid, tok3, tok3, dst_s.reshape(ntile, 1, tm), w_s.reshape(p_rows, 1), xn2t, wgu, wd)


def _final_kernel(x_ref, o0_ref, o1_ref, g_ref, y_ref):
    tm, d = x_ref.shape
    x = x_ref[...] + _rows_from_tiles(o0_ref, tm, d) + _rows_from_tiles(o1_ref, tm, d)
    y_ref[...] = _rmsnorm(x, g_ref[...])


def _final(x, moe_out, g, *, tm):
    n, d = x.shape
    nrt = n // tm
    rt = V7X_SUBLANES
    return pl.pallas_call(
        _final_kernel,
        grid=(nrt,),
        in_specs=[pl.BlockSpec((tm, d), lambda i: (i, 0)),
                  pl.BlockSpec((tm * rt, V7X_LANES), lambda i: (i, 0)),
                  pl.BlockSpec((tm * rt, V7X_LANES), lambda i: (i + nrt, 0)),
                  pl.BlockSpec((1, d), lambda i: (0, 0))],
        out_specs=pl.BlockSpec((tm, d), lambda i: (i, 0)),
        out_shape=jax.ShapeDtypeStruct((n, d), F32),
        compiler_params=pltpu.CompilerParams(dimension_semantics=("parallel",)),
        name="final_norm",
    )(x, moe_out, moe_out, g)


def _block_diag_tiles(w):
    nb, bw, _ = w.shape
    per = V7X_MXU_DIM // bw
    w4 = w.reshape(nb // per, per, bw, bw)
    eye = jnp.eye(per, dtype=w.dtype)
    return jnp.einsum("gicd,ij->gicjd", w4, eye).reshape(nb // per, per * bw, per * bw)


def _layer_params(l, g_mix, w_in, b_igate, b_fgate, conv_w, conv_b, lru_wr, lru_br, lru_wi, lru_bi,
                  lru_lambda, m_gn, w_pa, w_pb, w_out, g_ffn, w_group, b_group, w_erouter, b_erouter,
                  w_gate, w_up, w_down):
    d = w_in.shape[1]
    cc = conv_w.shape[2]
    heads = b_igate.shape[1]
    dm = m_gn.shape[1]
    assert dm == d and lru_lambda.shape[1] == d and cc == 3 * d
    w = w_in[l]
    v0, o0, gate0 = cc, cc + dm, cc + 2 * dm
    w_z = jnp.concatenate([w[:, :v0], w[:, o0:gate0], w[:, gate0 + 2 * heads:]], axis=1).astype(BF16)
    w_v = w[:, v0:o0].astype(BF16)
    w_gates = jnp.pad(w[:, gate0:gate0 + 2 * heads], ((0, 0), (0, GATE_LANES - 2 * heads))).astype(BF16)
    gate_bias = jnp.pad(jnp.concatenate([b_igate[l], b_fgate[l]]), (0, GATE_LANES - 2 * heads))[None]
    wri = jnp.concatenate([_block_diag_tiles(lru_wr[l]), _block_diag_tiles(lru_wi[l])], axis=2).astype(BF16)
    ng, _, ne = w_erouter.shape[1:]
    wr = jnp.concatenate([w_group[l], jnp.transpose(w_erouter[l], (1, 0, 2)).reshape(d, ng * ne)], axis=1)
    br = jnp.concatenate([b_group[l], b_erouter[l].reshape(ng * ne)])
    rpad = ROUTE_ROWS - wr.shape[1]
    wrt = jnp.pad(wr, ((0, 0), (0, rpad))).T
    wrt_hi = wrt.astype(BF16)
    wrt_lo = (wrt - wrt_hi.astype(F32)).astype(BF16)
    return dict(
        g_mix=g_mix[l][None], w_z=w_z, w_v=w_v, w_gates=w_gates, gate_bias=gate_bias,
        conv_w=conv_w[l], conv_b=conv_b[l][None], wri=wri,
        b_r=lru_br[l][None], b_i=lru_bi[l][None], lam=lru_lambda[l][None], m_gn=m_gn[l][None],
        w_pa=w_pa[l].astype(BF16), w_pb=w_pb[l].astype(BF16), w_out=w_out[l].astype(BF16),
        g_ffn=g_ffn[l][None], wrt_hi=wrt_hi, wrt_lo=wrt_lo, brt=jnp.pad(br, (0, rpad))[:, None],
        wgu=jnp.concatenate([w_gate[l], w_up[l]], axis=2).astype(BF16), wd=w_down[l].astype(BF16),
    )


def _tile_sizes(n, seq, n_exp):
    t = min(256, seq)
    chunk = min(128, t)
    tm_first = min(1024, n)
    tm = min(512, n)
    tm_exp = 512 if 2 * n >= 8 * n_exp * 512 else 128
    return t, chunk, tm_first, tm, tm_exp


def _trunk(x, conv0, h00, c00, n00, m00, layers, g_final):
    batch, seq, d = x.shape
    n = batch * seq
    depth = len(layers)
    t, chunk, tm_first, tm, tm_exp = _tile_sizes(n, seq, layers[0]["wd"].shape[0])
    xf = x.reshape(n, d)
    moe_out = None
    convs, hs, cs, ns, ms = [], [], [], [], []
    heads = n00.shape[2]
    rt = V7X_SUBLANES
    for l in range(depth):
        p = layers[l]
        if moe_out is None:
            tm_in, col_blocks = tm_first, 6
        else:
            tm_in, col_blocks = tm, 3
        xf, z, vt, zg = _inproj(xf, moe_out, p, tm=tm_in, tn=p["w_z"].shape[1] // col_blocks, t=t)
        conv0p = jnp.pad(conv0[l], ((0, 0), (rt - (CONV_W - 1), 0), (0, 0)))
        m0p = jnp.broadcast_to(
            jnp.pad(m00[l], ((0, 0), (0, rt - heads)))[:, :, None], (batch, rt, V7X_LANES))
        (xf, xn2t, rid, rw, conv_s, h_last, c_new, n_new, m_new) = _mixer(
            z, vt, zg, xf, conv0p, h00[l][:, None, :], c00[l], n00[l], m0p, p,
            batch=batch, seq=seq, t=t, chunk=chunk)
        e_flat = jnp.concatenate([rid[:, 0, :].reshape(n), rid[:, 1, :].reshape(n)])
        w_flat = jnp.concatenate([rw[:, 0, :].reshape(n), rw[:, 1, :].reshape(n)])
        moe_out = _experts(xn2t, e_flat, w_flat, p["wgu"], p["wd"], n_tokens=n, tm=tm_exp)
        convs.append(conv_s[:, rt - (CONV_W - 1):, :])
        hs.append(h_last[:, 0, :])
        cs.append(c_new)
        ns.append(n_new)
        ms.append(m_new[:, :heads, 0])
    y = _final(xf, moe_out, g_final[None], tm=tm).reshape(batch, seq, d)
    return y, jnp.stack(convs), jnp.stack(hs), jnp.stack(cs), jnp.stack(ns), jnp.stack(ms)


def kernel(x_prompt, x_sample, state_conv, state_lru, state_mlstm_C, state_mlstm_n, state_mlstm_m,
           g_mix, w_in, b_igate, b_fgate, conv_w, conv_b, lru_wr, lru_br, lru_wi, lru_bi, lru_lambda,
           m_gn, w_pa, w_pb, w_out, g_ffn, w_group, b_group, w_erouter, b_erouter,
           w_gate, w_up, w_down, g_final):
    depth = w_in.shape[0]
    layers = [
        _layer_params(l, g_mix, w_in, b_igate, b_fgate, conv_w, conv_b, lru_wr, lru_br, lru_wi, lru_bi,
                      lru_lambda, m_gn, w_pa, w_pb, w_out, g_ffn, w_group, b_group, w_erouter,
                      b_erouter, w_gate, w_up, w_down)
        for l in range(depth)]
    dt = x_prompt.dtype
    bp = x_prompt.shape[0]
    zeros_like_state = lambda s: jnp.zeros((depth, bp) + s.shape[2:], dt)
    y_s, s_conv, s_lru, s_c, s_n, s_m = _trunk(
        x_sample, state_conv, state_lru, state_mlstm_C, state_mlstm_n, state_mlstm_m,
        layers, g_final)
    y_p, p_conv, p_lru, p_c, p_n, p_m = _trunk(
        x_prompt, zeros_like_state(state_conv), zeros_like_state(state_lru),
        zeros_like_state(state_mlstm_C), zeros_like_state(state_mlstm_n),
        zeros_like_state(state_mlstm_m), layers, g_final)
    return (y_p, y_s, p_conv, p_lru, p_c, p_n, p_m, s_conv, s_lru, s_c, s_n, s_m)
```

```python
import functools

import jax
import jax.numpy as jnp
from jax import lax
from jax.experimental import pallas as pl
from jax.experimental.pallas import tpu as pltpu

F32 = jnp.float32
BF16 = jnp.bfloat16

EPS = 1e-6
LRU_C = 8.0
CONV_W = 4
N_GROUPS = 4
EXP_PER_GROUP = 4

V7X_VMEM_BYTES = 64 * 2**20
V7X_LANES = 128
V7X_SUBLANES = 8
V7X_MXU_DIM = 256

VMEM_LIMIT_BYTES = V7X_VMEM_BYTES - 8 * 2**20

ROUTE_ROWS = 32
GATE_LANES = V7X_LANES


def _rmsnorm(x, g):
    return (x * lax.rsqrt(jnp.mean(x * x, axis=-1, keepdims=True) + EPS)) * g


def _sigmoid(x):
    return 0.5 * jnp.tanh(0.5 * x) + 0.5


def _silu(x):
    h = 0.5 * x
    return h * jnp.tanh(h) + h


def _log_sigmoid(x):
    return jnp.minimum(x, 0.0) - jnp.log1p(jnp.exp(-jnp.abs(x)))


def _softplus(x):
    return jnp.maximum(x, 0.0) + jnp.log1p(jnp.exp(-jnp.abs(x)))


def _dot(a, b):
    return jnp.dot(a, b, preferred_element_type=F32)


def _dot_nt(a, b, precision=None):
    return lax.dot_general(a, b, (((1,), (1,)), ((), ())), precision=precision,
                           preferred_element_type=F32)


def _split_bf16(x):
    hi = x.astype(BF16)
    return hi, (x - hi.astype(F32)).astype(BF16)


def _rows_from_tiles(ref, n_rows, d):
    return jnp.concatenate(
        [ref[pl.ds(j, n_rows, stride=V7X_SUBLANES), :] for j in range(d // V7X_LANES)], axis=1)


def _rows_to_tiles(ref, x):
    n_rows, d = x.shape
    for j in range(d // V7X_LANES):
        ref[pl.ds(j, n_rows, stride=V7X_SUBLANES), :] = x[:, j * V7X_LANES:(j + 1) * V7X_LANES]


def _inproj_kernel(*refs, with_moe, t):
    if with_moe:
        x_ref, o0_ref, o1_ref, g_ref, w_ref, wv_ref, wg_ref, xres_ref, z_ref, vt_ref, zg_ref, xn_s = refs
    else:
        x_ref, g_ref, w_ref, wv_ref, wg_ref, z_ref, vt_ref, zg_ref, xn_s = refs
    tm, d = x_ref.shape
    j = pl.program_id(1)

    @pl.when(j == 0)
    def _():
        x = x_ref[...]
        if with_moe:
            x = x + _rows_from_tiles(o0_ref, tm, d) + _rows_from_tiles(o1_ref, tm, d)
            xres_ref[...] = x
        xn_s[...] = _rmsnorm(x, g_ref[...]).astype(BF16)
        zg_ref[...] = _dot(xn_s[...], wg_ref[...])
        v = _dot(xn_s[...], wv_ref[...])
        tp = -(-t // V7X_LANES) * V7X_LANES
        for q in range(tm // t):
            vq = v[q * t:(q + 1) * t, :]
            if tp != t:
                vq = jnp.concatenate([vq, jnp.zeros((tp - t, d), F32)], axis=0)
            vt_ref[q] = vq.T[:, :t]

    @pl.when(j > 0)
    def _():
        z_ref[...] = _dot(xn_s[...], w_ref[...])


def _inproj(x, moe_out, p, *, tm, tn, t):
    n, d = x.shape
    zc = p["w_z"].shape[1]
    nrt = n // tm
    rt = V7X_SUBLANES
    parts = [x]
    part_specs = [pl.BlockSpec((tm, d), lambda i, j: (i, 0))]
    if moe_out is not None:
        parts += [moe_out, moe_out]
        part_specs += [pl.BlockSpec((tm * rt, V7X_LANES), lambda i, j: (i, 0)),
                       pl.BlockSpec((tm * rt, V7X_LANES), lambda i, j: (i + nrt, 0))]
    out_shape = [jax.ShapeDtypeStruct((n, zc), F32),
                 jax.ShapeDtypeStruct((n // t, d, t), F32),
                 jax.ShapeDtypeStruct((n, GATE_LANES), F32)]
    zcol = lambda j: jnp.maximum(j - 1, 0)
    out_specs = [pl.BlockSpec((tm, tn), lambda i, j: (i, zcol(j))),
                 pl.BlockSpec((tm // t, d, t), lambda i, j: (i, 0, 0)),
                 pl.BlockSpec((tm, GATE_LANES), lambda i, j: (i, 0))]
    if moe_out is not None:
        out_shape = [jax.ShapeDtypeStruct((n, d), F32)] + out_shape
        out_specs = [pl.BlockSpec((tm, d), lambda i, j: (i, 0))] + out_specs
    res = pl.pallas_call(
        functools.partial(_inproj_kernel, with_moe=moe_out is not None, t=t),
        grid=(nrt, zc // tn + 1),
        in_specs=part_specs + [pl.BlockSpec((1, d), lambda i, j: (0, 0)),
                               pl.BlockSpec((d, tn), lambda i, j: (0, zcol(j))),
                               pl.BlockSpec((d, d), lambda i, j: (0, 0)),
                               pl.BlockSpec((d, GATE_LANES), lambda i, j: (0, 0))],
        out_specs=out_specs,
        out_shape=out_shape,
        scratch_shapes=[pltpu.VMEM((tm, d), BF16)],
        compiler_params=pltpu.CompilerParams(
            dimension_semantics=("parallel", "arbitrary"),
            vmem_limit_bytes=VMEM_LIMIT_BYTES),
        name="inproj",
    )(*parts, p["g_mix"], p["w_z"], p["w_v"], p["w_gates"])
    if moe_out is not None:
        return res
    return [x] + list(res)


def _route(xn2, wrh_ref, wrl_ref, brt_ref, rid_ref, rw_ref):
    t = xn2.shape[0]
    xh, xl = _split_bf16(xn2)
    logit = (_dot_nt(wrh_ref[...], xh) + _dot_nt(wrh_ref[...], xl) + _dot_nt(wrl_ref[...], xh)
             + brt_ref[...])
    ng, ne = N_GROUPS, EXP_PER_GROUP
    gl = logit[0:ng]
    gidx = lax.broadcasted_iota(jnp.int32, (ng, t), 0).astype(F32)
    idx = lax.broadcasted_iota(jnp.int32, (ne, t), 0).astype(F32)
    gmax = jnp.max(gl, axis=0, keepdims=True)
    g_idx = jnp.min(jnp.where(gl == gmax, gidx, float(ng)), axis=0, keepdims=True)
    p_g = 1.0 / jnp.sum(jnp.exp(gl - gmax), axis=0, keepdims=True)
    el = jnp.zeros((ne, t), F32)
    for g in range(ng):
        el = jnp.where(g_idx == float(g), logit[ng + g * ne:ng + (g + 1) * ne], el)
    v1 = jnp.max(el, axis=0, keepdims=True)
    i1 = jnp.min(jnp.where(el == v1, idx, float(ne)), axis=0, keepdims=True)
    rest = jnp.where(idx == i1, -jnp.inf, el)
    v2 = jnp.max(rest, axis=0, keepdims=True)
    i2 = jnp.min(jnp.where(rest == v2, idx, float(ne)), axis=0, keepdims=True)
    e21 = jnp.exp(v2 - v1)
    w1 = (1.0 / (1.0 + e21)) * p_g
    w2 = (e21 / (1.0 + e21)) * p_g
    e1 = g_idx * float(ne) + i1
    e2 = g_idx * float(ne) + i2
    slot = lax.broadcasted_iota(jnp.int32, (V7X_SUBLANES, t), 0)
    rid_ref[0] = jnp.where(slot == 0, e1, jnp.where(slot == 1, e2, 0.0)).astype(jnp.int32)
    rw_ref[0] = jnp.where(slot == 0, w1, jnp.where(slot == 1, w2, 0.0))


def _mixer_kernel(*refs, chunk, heads, nt, n_tiles):
    wrh_ref, wrl_ref, brt_ref = refs[24:27]
    rid_ref, rw_ref = refs[29:31]
    xp_s = refs[-1]
    s = pl.program_id(0)

    @pl.when(s == 0)
    def _():
        xp_s[...] = jnp.zeros(xp_s.shape, F32)

    @pl.when(s < n_tiles)
    def _():
        _mixer_tile(*refs, chunk=chunk, heads=heads, step=s % nt, last_step=nt - 1)

    @pl.when(s == n_tiles)
    def _():
        _route(xp_s[...], wrh_ref, wrl_ref, brt_ref, rid_ref, rw_ref)


def _mixer_tile(zc_ref, zo_ref, zga_ref, zgb_ref, vt_ref, zg_ref, x_ref,
                conv0_ref, h0_ref, c0_ref, n0_ref, m0_ref,
                cw_ref, cb_ref, wri_ref, br_ref, bi_ref, lam_ref, gbias_ref, mgn_ref,
                wpa_ref, wpb_ref, wout_ref, gffn_ref, wrh_ref, wrl_ref, brt_ref,
                xnew_ref, xn2_ref, rid_ref, rw_ref,
                convs_ref, hlast_ref, cout_ref, nout_ref, mout_ref,
                zbuf, xl_s, q_s, k_s, a_s, u_s, h_s, hm_s, c_s, n_s, m_s, hc_s, xp_s,
                *, chunk, heads, step, last_step):
    t = x_ref.shape[0]
    d = x_ref.shape[1]
    hd = d // heads
    tail = V7X_SUBLANES

    @pl.when(step == 0)
    def _init():
        zbuf[0:tail, :] = conv0_ref[0]
        hc_s[...] = h0_ref[0]
        c_s[...] = c0_ref[0]
        n_s[...] = n0_ref[0]
        m_s[...] = m0_ref[0]

    zbuf[tail:tail + t, :] = zc_ref[...]

    lead_rows = lax.broadcasted_iota(jnp.int32, (tail, d), 0)

    def conv_cols(lo, hi):
        cur = zbuf[tail:tail + t, lo:hi]
        prev = zbuf[0:tail, lo:hi]
        acc = None
        for tap in range(CONV_W):
            s = CONV_W - 1 - tap
            if s == 0:
                term = cur
            else:
                rolled = pltpu.roll(cur, s, 0)
                head = jnp.where(lead_rows < s, pltpu.roll(prev, s, 0), rolled[0:tail])
                term = jnp.concatenate([head, rolled[tail:]], axis=0) if t > tail else head
            term = term * cw_ref[tap:tap + 1, lo:hi]
            acc = term if acc is None else acc + term
        return acc + cb_ref[:, lo:hi]

    q_s[...] = _silu(conv_cols(d, 2 * d)).astype(BF16)
    k_s[...] = (_silu(conv_cols(2 * d, 3 * d)) * (hd ** -0.5)).astype(BF16)

    tc = chunk
    gates = zg_ref[...] + gbias_ref[...]
    lf = _log_sigmoid(gates)
    tri_r = lax.broadcasted_iota(jnp.int32, (tc, tc), 0)
    tri_c = lax.broadcasted_iota(jnp.int32, (tc, tc), 1)
    causal = tri_c <= tri_r
    tri = causal.astype(F32)
    pick = (lax.broadcasted_iota(jnp.int32, (V7X_SUBLANES, GATE_LANES), 0) ==
            lax.broadcasted_iota(jnp.int32, (V7X_SUBLANES, GATE_LANES), 1)).astype(F32)
    for j in range(t // tc):
        r0, r1 = j * tc, (j + 1) * tc
        bcum = jnp.dot(tri, lf[r0:r1], precision=lax.Precision.HIGHEST,
                       preferred_element_type=F32)
        b_al = pltpu.roll(bcum, GATE_LANES - heads, 1)
        g_al = gates[r0:r1] - b_al
        g_rows = _dot_nt(pick, g_al, precision=lax.Precision.HIGHEST)
        for h in range(heads):
            c0, c1 = h * hd, (h + 1) * hd
            qh = q_s[r0:r1, c0:c1]
            kh = k_s[r0:r1, c0:c1]
            vth = vt_ref[0, c0:c1, r0:r1].astype(BF16)
            bcol = b_al[:, h:h + 1]
            gcol = g_al[:, h:h + 1]
            m_prev = m_s[h:h + 1, 0:1]
            dlog = jnp.where(causal, bcol + g_rows[h:h + 1, :], -jnp.inf)
            m_inter = bcol + m_prev
            m_t = jnp.maximum(m_inter, jnp.max(dlog, axis=-1, keepdims=True))
            s = _dot_nt(qh, kh) * jnp.exp(dlog - m_t)
            inter = jnp.exp(m_inter - m_t)
            qc_ = _dot_nt(qh, c_s[h].astype(BF16))
            num = _dot_nt(s.astype(BF16), vth) + inter * qc_
            qn = jnp.sum(qh.astype(F32) * n_s[h:h + 1, :], axis=-1, keepdims=True)
            den = jnp.sum(s, axis=-1, keepdims=True) + inter * qn
            hm_s[r0:r1, c0:c1] = num / jnp.maximum(jnp.abs(den), jnp.exp(-m_t))
            b_end = bcol[tc - 1:tc, :]
            m_new = m_t[tc - 1:tc, :]
            wk = jnp.exp(b_end + gcol - m_new)
            decay = jnp.exp(b_end + m_prev - m_new)
            kw = wk * kh.astype(F32)
            c_s[h] = decay * c_s[h] + _dot(vth, kw.astype(BF16))
            n_s[h:h + 1, :] = decay * n_s[h:h + 1, :] + jnp.sum(kw, axis=0, keepdims=True)
            m_s[h:h + 1, :] = jnp.broadcast_to(m_new, (1, V7X_LANES))

    _route(xp_s[...], wrh_ref, wrl_ref, brt_ref, rid_ref, rw_ref)

    xl_s[...] = conv_cols(0, d)
    zbuf[0:tail, :] = zbuf[t:t + tail, :]
    gw = V7X_MXU_DIM
    sp = _softplus(-lam_ref[...])
    for g in range(d // gw):
        lo, hi = g * gw, (g + 1) * gw
        xg = xl_s[:, lo:hi]
        ri = _dot(xg.astype(BF16), wri_ref[g])
        r = _sigmoid(ri[:, :gw] + br_ref[:, lo:hi])
        ig = _sigmoid(ri[:, gw:] + bi_ref[:, lo:hi])
        log_a = (-LRU_C * r) * sp[:, lo:hi]
        a = jnp.exp(log_a)
        sq = -jnp.tanh(log_a) * (a * a + 1.0)
        mult = jnp.where(sq > 0.0, sq * lax.rsqrt(sq), 0.0)
        a_s[:, lo:hi] = a
        u_s[:, lo:hi] = mult * (ig * xg)

    rows = lax.broadcasted_iota(jnp.int32, (V7X_SUBLANES, d), 0)
    shifts = (1, 2, 4)
    masks = [rows >= s for s in shifts]
    hc = hc_s[...]
    for gi in range(t // V7X_SUBLANES):
        r0 = gi * V7X_SUBLANES
        a = a_s[r0:r0 + V7X_SUBLANES, :]
        u = u_s[r0:r0 + V7X_SUBLANES, :]
        for s, mk in zip(shifts, masks):
            a_sh = jnp.where(mk, pltpu.roll(a, s, 0), 1.0)
            u_sh = jnp.where(mk, pltpu.roll(u, s, 0), 0.0)
            u = a * u_sh + u
            a = a * a_sh
        hg = u + a * hc
        h_s[r0:r0 + V7X_SUBLANES, :] = hg
        hc = hg[V7X_SUBLANES - 1:V7X_SUBLANES, :]
    hc_s[...] = hc

    for h in range(heads):
        c0, c1 = h * hd, (h + 1) * hd
        seg = hm_s[:, c0:c1]
        seg = seg * lax.rsqrt(jnp.mean(seg * seg, axis=-1, keepdims=True) + EPS)
        hm_s[:, c0:c1] = (seg * mgn_ref[:, c0:c1]) * _sigmoid(zo_ref[:, c0:c1])
    pa = _dot(h_s[...].astype(BF16), wpa_ref[...])
    pb = _dot(hm_s[...].astype(BF16), wpb_ref[...])
    merged = _sigmoid(zga_ref[...]) * pa + _sigmoid(zgb_ref[...]) * pb
    xnew = x_ref[...] + _dot(merged.astype(BF16), wout_ref[...])
    xnew_ref[...] = xnew
    xn2 = _rmsnorm(xnew, gffn_ref[...])
    _rows_to_tiles(xn2_ref, xn2)
    xp_s[...] = xn2

    @pl.when(step == last_step)
    def _final_state():
        convs_ref[0] = zbuf[0:tail, :]
        hlast_ref[0] = hc_s[...]
        cout_ref[0] = c_s[...]
        nout_ref[0] = n_s[...]
        mout_ref[0] = m_s[...]


def _mixer(z, vt, zg, x, conv0p, h0, c0, n0, m0p, p, *, batch, seq, t, chunk):
    n, d = x.shape
    heads = n0.shape[1]
    hd = n0.shape[2]
    nt = seq // t
    cc = conv0p.shape[2]
    rt = V7X_SUBLANES
    n_tiles = batch * nt
    tile = lambda s: jnp.minimum(s, n_tiles - 1)
    zblk = lambda width, col: pl.BlockSpec((t, width), lambda s: (tile(s), col))
    const = lambda shape: pl.BlockSpec(shape, lambda s: (0,) * len(shape))
    per_b = lambda shape: pl.BlockSpec((1,) + shape, lambda s: (tile(s) // nt,) + (0,) * len(shape))
    tok = pl.BlockSpec((t, d), lambda s: (tile(s), 0))
    in_specs = [
        zblk(cc, 0), zblk(d, cc // d), zblk(d, cc // d + 1), zblk(d, cc // d + 2),
        pl.BlockSpec((1, d, t), lambda s: (tile(s), 0, 0)),
        zblk(GATE_LANES, 0), tok,
        per_b((rt, cc)), per_b((1, d)), per_b((heads, hd, hd)), per_b((heads, hd)),
        per_b((rt, V7X_LANES)),
        const((CONV_W, cc)), const((1, cc)), const(p["wri"].shape), const((1, d)), const((1, d)),
        const((1, d)), const((1, GATE_LANES)), const((1, d)),
        const((d, d)), const((d, d)), const((d, d)), const((1, d)),
        const((ROUTE_ROWS, d)), const((ROUTE_ROWS, d)), const((ROUTE_ROWS, 1)),
    ]
    route_blk = pl.BlockSpec((1, rt, t), lambda s: (jnp.maximum(s - 1, 0), 0, 0))
    out_specs = (
        tok, pl.BlockSpec((t * rt, V7X_LANES), lambda s: (tile(s), 0)), route_blk, route_blk,
        per_b((rt, cc)), per_b((1, d)), per_b((heads, hd, hd)), per_b((heads, hd)),
        per_b((rt, V7X_LANES)),
    )
    out_shape = (
        jax.ShapeDtypeStruct((n, d), F32), jax.ShapeDtypeStruct((n * rt, V7X_LANES), F32),
        jax.ShapeDtypeStruct((batch * nt, rt, t), jnp.int32),
        jax.ShapeDtypeStruct((batch * nt, rt, t), F32),
        jax.ShapeDtypeStruct((batch, rt, cc), F32),
        jax.ShapeDtypeStruct((batch, 1, d), F32),
        jax.ShapeDtypeStruct((batch, heads, hd, hd), F32),
        jax.ShapeDtypeStruct((batch, heads, hd), F32),
        jax.ShapeDtypeStruct((batch, rt, V7X_LANES), F32),
    )
    scratch = [
        pltpu.VMEM((t + rt, cc), F32),
        pltpu.VMEM((t, d), F32),
        pltpu.VMEM((t, d), BF16),
        pltpu.VMEM((t, d), BF16),
        pltpu.VMEM((t, d), F32),
        pltpu.VMEM((t, d), F32),
        pltpu.VMEM((t, d), F32),
        pltpu.VMEM((t, d), F32),
        pltpu.VMEM((heads, hd, hd), F32),
        pltpu.VMEM((heads, hd), F32),
        pltpu.VMEM((rt, V7X_LANES), F32),
        pltpu.VMEM((1, d), F32),
        pltpu.VMEM((t, d), F32),
    ]
    return pl.pallas_call(
        functools.partial(_mixer_kernel, chunk=chunk, heads=heads, nt=nt, n_tiles=n_tiles),
        grid=(n_tiles + 1,),
        in_specs=in_specs,
        out_specs=out_specs,
        out_shape=out_shape,
        scratch_shapes=scratch,
        compiler_params=pltpu.CompilerParams(
            dimension_semantics=("arbitrary",),
            vmem_limit_bytes=VMEM_LIMIT_BYTES),
        name="mixer",
    )(z, z, z, z, vt, zg, x, conv0p, h0, c0, n0, m0p,
      p["conv_w"], p["conv_b"], p["wri"], p["b_r"], p["b_i"], p["lam"], p["gate_bias"], p["m_gn"],
      p["w_pa"], p["w_pb"], p["w_out"], p["g_ffn"], p["wrt_hi"], p["wrt_lo"], p["brt"])


def _expert_kernel(te_ref, nv_ref, tok_ref, tokn_ref, dst_ref, wcol_ref, xn_hbm, wgu_ref, wd_ref, o_hbm,
                   xbuf, ybuf, sem_in, sem_out):
    del te_ref
    j = pl.program_id(0)
    nv = nv_ref[0]
    rt = V7X_SUBLANES
    tm = wcol_ref.shape[0]
    d = wgu_ref.shape[1]
    de = wd_ref.shape[1]
    slot = j % 2

    def row_in(tok, r, sl):
        return pltpu.make_async_copy(xn_hbm.at[pl.ds(pl.multiple_of(tok, rt), rt)],
                                     xbuf.at[sl, pl.ds(r * rt, rt)], sem_in.at[sl])

    def row_out(dst, r, sl):
        return pltpu.make_async_copy(ybuf.at[sl, pl.ds(r * rt, rt)],
                                     o_hbm.at[pl.ds(pl.multiple_of(dst, rt), rt)], sem_out.at[sl])

    def gather(idx_ref, sl):
        for r in range(tm):
            row_in(idx_ref[0, 0, r], r, sl).start()

    def gather_wait(sl):
        for r in range(tm):
            row_in(0, r, sl).wait()

    def scatter(sl):
        for r in range(tm):
            row_out(dst_ref[0, 0, r], r, sl).start()

    def scatter_wait(sl):
        for r in range(tm):
            row_out(0, r, sl).wait()

    @pl.when(j == 0)
    def _zero_padding_rows():
        pad0 = o_hbm.shape[0] - 2 * tm * rt
        ybuf[1] = jnp.zeros(ybuf.shape[1:], F32)
        fills = [pltpu.make_async_copy(ybuf.at[1], o_hbm.at[pl.ds(pad0 + half * tm * rt, tm * rt)],
                                       sem_out.at[1]) for half in range(2)]
        for cp in fills:
            cp.start()
        for cp in fills:
            cp.wait()

    @pl.when(j < nv)
    def _tile():
        @pl.when(j == 0)
        def _():
            gather(tok_ref, slot)

        @pl.when(j + 1 < nv)
        def _():
            gather(tokn_ref, 1 - slot)

        gather_wait(slot)

        x = _rows_from_tiles(xbuf.at[slot], tm, d).astype(BF16)
        gu = _dot(x, wgu_ref[0])
        hid = _silu(gu[:, :de]) * gu[:, de:]
        y = _dot(hid.astype(BF16), wd_ref[0]) * wcol_ref[...]
        _rows_to_tiles(ybuf.at[slot], y)

        scatter(slot)

        @pl.when(j >= 1)
        def _():
            scatter_wait(1 - slot)

        @pl.when(j == nv - 1)
        def _():
            scatter_wait(slot)


def _experts(xn2t, e_flat, w_flat, wgu, wd, *, n_tokens, tm):
    n = n_tokens
    n_exp, de, d = wd.shape
    na = e_flat.shape[0]
    rt = V7X_SUBLANES
    p_rows = na + n_exp * tm
    ntile = p_rows // tm

    cnt = jnp.sum((e_flat[:, None] == jnp.arange(n_exp, dtype=jnp.int32)[None, :]).astype(jnp.int32), axis=0)
    pad = (-cnt) % tm
    pad_key = jnp.where(lax.broadcasted_iota(jnp.int32, (n_exp, tm), 1) < pad[:, None],
                        lax.broadcasted_iota(jnp.int32, (n_exp, tm), 0), n_exp).reshape(n_exp * tm)
    keys = jnp.concatenate([e_flat, pad_key])
    wts = jnp.concatenate([w_flat, jnp.zeros((n_exp * tm,), F32)])
    shift = (p_rows - 1).bit_length()
    assert (n_exp + 1) << shift < 2**31
    pos = lax.iota(jnp.int32, p_rows)
    packed_s, w_s = lax.sort((keys * (1 << shift) + pos, wts), num_keys=1)
    key_s = packed_s >> shift
    src_s = packed_s & ((1 << shift) - 1)
    is_pad = src_s >= na
    tok_s = jnp.where(is_pad, 0, jnp.where(src_s >= n, src_s - n, src_s))
    dst_s = jnp.where(is_pad, na + ((pos // tm) % 2) * tm + pos % tm, src_s)
    tile_key = key_s[::tm]
    n_valid = jnp.sum((tile_key < n_exp).astype(jnp.int32)).reshape(1)
    tile_e = jnp.minimum(tile_key, n_exp - 1)

    idx_blk = lambda fn: pl.BlockSpec((1, 1, tm), fn, memory_space=pltpu.SMEM)
    grid_spec = pltpu.PrefetchScalarGridSpec(
        num_scalar_prefetch=2,
        grid=(ntile,),
        in_specs=[
            idx_blk(lambda j, te, nv: (j, 0, 0)),
            idx_blk(lambda j, te, nv: (jnp.minimum(j + 1, ntile - 1), 0, 0)),
            idx_blk(lambda j, te, nv: (j, 0, 0)),
            pl.BlockSpec((tm, 1), lambda j, te, nv: (j, 0)),
            pl.BlockSpec(memory_space=pl.ANY),
            pl.BlockSpec((1, d, 2 * de), lambda j, te, nv: (te[j], 0, 0)),
            pl.BlockSpec((1, de, d), lambda j, te, nv: (te[j], 0, 0)),
        ],
        out_specs=pl.BlockSpec(memory_space=pl.ANY),
        scratch_shapes=[
            pltpu.VMEM((2, tm * rt, V7X_LANES), F32), pltpu.VMEM((2, tm * rt, V7X_LANES), F32),
            pltpu.SemaphoreType.DMA((2,)), pltpu.SemaphoreType.DMA((2,)),
        ],
    )
    tok3 = (tok_s * rt).reshape(ntile, 1, tm)
    dst_s = dst_s * rt
    return pl.pallas_call(
        _expert_kernel,
        grid_spec=grid_spec,
        out_shape=jax.ShapeDtypeStruct(((na + 2 * tm) * rt, V7X_LANES), F32),
        compiler_params=pltpu.CompilerParams(
            dimension_semantics=("arbitrary",),
            vmem_limit_bytes=VMEM_LIMIT_BYTES),
        name="experts",
    )(tile_e, n_valid, tok3, tok3, dst_s.reshape(ntile, 1, tm), w_s.reshape(p_rows, 1), xn2t, wgu, wd)


def _final_kernel(x_ref, o0_ref, o1_ref, g_ref, y_ref):
    tm, d = x_ref.shape
    x = x_ref[...] + _rows_from_tiles(o0_ref, tm, d) + _rows_from_tiles(o1_ref, tm, d)
    y_ref[...] = _rmsnorm(x, g_ref[...])


def _final(x, moe_out, g, *, tm):
    n, d = x.shape
    nrt = n // tm
    rt = V7X_SUBLANES
    return pl.pallas_call(
        _final_kernel,
        grid=(nrt,),
        in_specs=[pl.BlockSpec((tm, d), lambda i: (i, 0)),
                  pl.BlockSpec((tm * rt, V7X_LANES), lambda i: (i, 0)),
                  pl.BlockSpec((tm * rt, V7X_LANES), lambda i: (i + nrt, 0)),
                  pl.BlockSpec((1, d), lambda i: (0, 0))],
        out_specs=pl.BlockSpec((tm, d), lambda i: (i, 0)),
        out_shape=jax.ShapeDtypeStruct((n, d), F32),
        compiler_params=pltpu.CompilerParams(dimension_semantics=("parallel",)),
        name="final_norm",
    )(x, moe_out, moe_out, g)


def _block_diag_tiles(w):
    nb, bw, _ = w.shape
    per = V7X_MXU_DIM // bw
    w4 = w.reshape(nb // per, per, bw, bw)
    eye = jnp.eye(per, dtype=w.dtype)
    return jnp.einsum("gicd,ij->gicjd", w4, eye).reshape(nb // per, per * bw, per * bw)


def _layer_params(l, g_mix, w_in, b_igate, b_fgate, conv_w, conv_b, lru_wr, lru_br, lru_wi, lru_bi,
                  lru_lambda, m_gn, w_pa, w_pb, w_out, g_ffn, w_group, b_group, w_erouter, b_erouter,
                  w_gate, w_up, w_down):
    d = w_in.shape[1]
    cc = conv_w.shape[2]
    heads = b_igate.shape[1]
    dm = m_gn.shape[1]
    assert dm == d and lru_lambda.shape[1] == d and cc == 3 * d
    w = w_in[l]
    v0, o0, gate0 = cc, cc + dm, cc + 2 * dm
    w_z = jnp.concatenate([w[:, :v0], w[:, o0:gate0], w[:, gate0 + 2 * heads:]], axis=1).astype(BF16)
    w_v = w[:, v0:o0].astype(BF16)
    w_gates = jnp.pad(w[:, gate0:gate0 + 2 * heads], ((0, 0), (0, GATE_LANES - 2 * heads))).astype(BF16)
    gate_bias = jnp.pad(jnp.concatenate([b_igate[l], b_fgate[l]]), (0, GATE_LANES - 2 * heads))[None]
    wri = jnp.concatenate([_block_diag_tiles(lru_wr[l]), _block_diag_tiles(lru_wi[l])], axis=2).astype(BF16)
    ng, _, ne = w_erouter.shape[1:]
    wr = jnp.concatenate([w_group[l], jnp.transpose(w_erouter[l], (1, 0, 2)).reshape(d, ng * ne)], axis=1)
    br = jnp.concatenate([b_group[l], b_erouter[l].reshape(ng * ne)])
    rpad = ROUTE_ROWS - wr.shape[1]
    wrt = jnp.pad(wr, ((0, 0), (0, rpad))).T
    wrt_hi = wrt.astype(BF16)
    wrt_lo = (wrt - wrt_hi.astype(F32)).astype(BF16)
    return dict(
        g_mix=g_mix[l][None], w_z=w_z, w_v=w_v, w_gates=w_gates, gate_bias=gate_bias,
        conv_w=conv_w[l], conv_b=conv_b[l][None], wri=wri,
        b_r=lru_br[l][None], b_i=lru_bi[l][None], lam=lru_lambda[l][None], m_gn=m_gn[l][None],
        w_pa=w_pa[l].astype(BF16), w_pb=w_pb[l].astype(BF16), w_out=w_out[l].astype(BF16),
        g_ffn=g_ffn[l][None], wrt_hi=wrt_hi, wrt_lo=wrt_lo, brt=jnp.pad(br, (0, rpad))[:, None],
        wgu=jnp.concatenate([w_gate[l], w_up[l]], axis=2).astype(BF16), wd=w_down[l].astype(BF16),
    )


def _tile_sizes(n, seq, n_exp):
    t = min(256, seq)
    chunk = min(128, t)
    tm_first = min(1024, n)
    tm = min(512, n)
    tm_exp = 512 if 2 * n >= 8 * n_exp * 512 else 128
    return t, chunk, tm_first, tm, tm_exp


def _trunk(x, conv0, h00, c00, n00, m00, layers, g_final):
    batch, seq, d = x.shape
    n = batch * seq
    depth = len(layers)
    t, chunk, tm_first, tm, tm_exp = _tile_sizes(n, seq, layers[0]["wd"].shape[0])
    xf = x.reshape(n, d)
    moe_out = None
    convs, hs, cs, ns, ms = [], [], [], [], []
    heads = n00.shape[2]
    rt = V7X_SUBLANES
    for l in range(depth):
        p = layers[l]
        if moe_out is None:
            tm_in, col_blocks = tm_first, 6
        else:
            tm_in, col_blocks = tm, 3
        xf, z, vt, zg = _inproj(xf, moe_out, p, tm=tm_in, tn=p["w_z"].shape[1] // col_blocks, t=t)
        conv0p = jnp.pad(conv0[l], ((0, 0), (rt - (CONV_W - 1), 0), (0, 0)))
        m0p = jnp.broadcast_to(
            jnp.pad(m00[l], ((0, 0), (0, rt - heads)))[:, :, None], (batch, rt, V7X_LANES))
        (xf, xn2t, rid, rw, conv_s, h_last, c_new, n_new, m_new) = _mixer(
            z, vt, zg, xf, conv0p, h00[l][:, None, :], c00[l], n00[l], m0p, p,
            batch=batch, seq=seq, t=t, chunk=chunk)
        e_flat = jnp.concatenate([rid[:, 0, :].reshape(n), rid[:, 1, :].reshape(n)])
        w_flat = jnp.concatenate([rw[:, 0, :].reshape(n), rw[:, 1, :].reshape(n)])
        moe_out = _experts(xn2t, e_flat, w_flat, p["wgu"], p["wd"], n_tokens=n, tm=tm_exp)
        convs.append(conv_s[:, rt - (CONV_W - 1):, :])
        hs.append(h_last[:, 0, :])
        cs.append(c_new)
        ns.append(n_new)
        ms.append(m_new[:, :heads, 0])
    y = _final(xf, moe_out, g_final[None], tm=tm).reshape(batch, seq, d)
    return y, jnp.stack(convs), jnp.stack(hs), jnp.stack(cs), jnp.stack(ns), jnp.stack(ms)


def kernel(x_prompt, x_sample, state_conv, state_lru, state_mlstm_C, state_mlstm_n, state_mlstm_m,
           g_mix, w_in, b_igate, b_fgate, conv_w, conv_b, lru_wr, lru_br, lru_wi, lru_bi, lru_lambda,
           m_gn, w_pa, w_pb, w_out, g_ffn, w_group, b_group, w_erouter, b_erouter,
           w_gate, w_up, w_down, g_final):
    depth = w_in.shape[0]
    layers = [
        _layer_params(l, g_mix, w_in, b_igate, b_fgate, conv_w, conv_b, lru_wr, lru_br, lru_wi, lru_bi,
                      lru_lambda, m_gn, w_pa, w_pb, w_out, g_ffn, w_group, b_group, w_erouter,
                      b_erouter, w_gate, w_up, w_down)
        for l in range(depth)]
    dt = x_prompt.dtype
    bp = x_prompt.shape[0]
    zeros_like_state = lambda s: jnp.zeros((depth, bp) + s.shape[2:], dt)
    y_s, s_conv, s_lru, s_c, s_n, s_m = _trunk(
        x_sample, state_conv, state_lru, state_mlstm_C, state_mlstm_n, state_mlstm_m,
        layers, g_final)
    y_p, p_conv, p_lru, p_c, p_n, p_m = _trunk(
        x_prompt, zeros_like_state(state_conv), zeros_like_state(state_lru),
        zeros_like_state(state_mlstm_C), zeros_like_state(state_mlstm_n),
        zeros_like_state(state_mlstm_m), layers, g_final)
    return (y_p, y_s, p_conv, p_lru, p_c, p_n, p_m, s_conv, s_lru, s_c, s_n, s_m)
```

```python
import functools

import jax
import jax.numpy as jnp
from jax import lax
from jax.experimental import pallas as pl
from jax.experimental.pallas import tpu as pltpu

F32 = jnp.float32
BF16 = jnp.bfloat16

EPS = 1e-6
LRU_C = 8.0
CONV_W = 4
N_GROUPS = 4
EXP_PER_GROUP = 4

V7X_VMEM_BYTES = 64 * 2**20
V7X_LANES = 128
V7X_SUBLANES = 8
V7X_MXU_DIM = 256

VMEM_LIMIT_BYTES = V7X_VMEM_BYTES - 8 * 2**20

ROUTE_ROWS = 32
GATE_LANES = V7X_LANES


def _rmsnorm(x, g):
    return (x * lax.rsqrt(jnp.mean(x * x, axis=-1, keepdims=True) + EPS)) * g


def _sigmoid(x):
    return 0.5 * jnp.tanh(0.5 * x) + 0.5


def _silu(x):
    h = 0.5 * x
    return h * jnp.tanh(h) + h


def _log_sigmoid(x):
    return jnp.minimum(x, 0.0) - jnp.log1p(jnp.exp(-jnp.abs(x)))


def _softplus(x):
    return jnp.maximum(x, 0.0) + jnp.log1p(jnp.exp(-jnp.abs(x)))


def _dot(a, b):
    return jnp.dot(a, b, preferred_element_type=F32)


def _dot_nt(a, b, precision=None):
    return lax.dot_general(a, b, (((1,), (1,)), ((), ())), precision=precision,
                           preferred_element_type=F32)


def _split_bf16(x):
    hi = x.astype(BF16)
    return hi, (x - hi.astype(F32)).astype(BF16)


def _rows_from_tiles(ref, n_rows, d):
    return jnp.concatenate(
        [ref[pl.ds(j, n_rows, stride=V7X_SUBLANES), :] for j in range(d // V7X_LANES)], axis=1)


def _rows_to_tiles(ref, x):
    n_rows, d = x.shape
    for j in range(d // V7X_LANES):
        ref[pl.ds(j, n_rows, stride=V7X_SUBLANES), :] = x[:, j * V7X_LANES:(j + 1) * V7X_LANES]


def _inproj_kernel(*refs, with_moe, t):
    if with_moe:
        x_ref, o0_ref, o1_ref, g_ref, w_ref, wv_ref, wg_ref, xres_ref, z_ref, vt_ref, zg_ref, xn_s = refs
    else:
        x_ref, g_ref, w_ref, wv_ref, wg_ref, z_ref, vt_ref, zg_ref, xn_s = refs
    tm, d = x_ref.shape
    j = pl.program_id(1)

    @pl.when(j == 0)
    def _():
        x = x_ref[...]
        if with_moe:
            x = x + _rows_from_tiles(o0_ref, tm, d) + _rows_from_tiles(o1_ref, tm, d)
            xres_ref[...] = x
        xn_s[...] = _rmsnorm(x, g_ref[...]).astype(BF16)
        zg_ref[...] = _dot(xn_s[...], wg_ref[...])
        v = _dot(xn_s[...], wv_ref[...])
        tp = -(-t // V7X_LANES) * V7X_LANES
        for q in range(tm // t):
            vq = v[q * t:(q + 1) * t, :]
            if tp != t:
                vq = jnp.concatenate([vq, jnp.zeros((tp - t, d), F32)], axis=0)
            vt_ref[q] = vq.T[:, :t]

    @pl.when(j > 0)
    def _():
        z_ref[...] = _dot(xn_s[...], w_ref[...])


def _inproj(x, moe_out, p, *, tm, tn, t):
    n, d = x.shape
    zc = p["w_z"].shape[1]
    nrt = n // tm
    rt = V7X_SUBLANES
    parts = [x]
    part_specs = [pl.BlockSpec((tm, d), lambda i, j: (i, 0))]
    if moe_out is not None:
        parts += [moe_out, moe_out]
        part_specs += [pl.BlockSpec((tm * rt, V7X_LANES), lambda i, j: (i, 0)),
                       pl.BlockSpec((tm * rt, V7X_LANES), lambda i, j: (i + nrt, 0))]
    out_shape = [jax.ShapeDtypeStruct((n, zc), F32),
                 jax.ShapeDtypeStruct((n // t, d, t), F32),
                 jax.ShapeDtypeStruct((n, GATE_LANES), F32)]
    zcol = lambda j: jnp.maximum(j - 1, 0)
    out_specs = [pl.BlockSpec((tm, tn), lambda i, j: (i, zcol(j))),
                 pl.BlockSpec((tm // t, d, t), lambda i, j: (i, 0, 0)),
                 pl.BlockSpec((tm, GATE_LANES), lambda i, j: (i, 0))]
    if moe_out is not None:
        out_shape = [jax.ShapeDtypeStruct((n, d), F32)] + out_shape
        out_specs = [pl.BlockSpec((tm, d), lambda i, j: (i, 0))] + out_specs
    res = pl.pallas_call(
        functools.partial(_inproj_kernel, with_moe=moe_out is not None, t=t),
        grid=(nrt, zc // tn + 1),
        in_specs=part_specs + [pl.BlockSpec((1, d), lambda i, j: (0, 0)),
                               pl.BlockSpec((d, tn), lambda i, j: (0, zcol(j))),
                               pl.BlockSpec((d, d), lambda i, j: (0, 0)),
                               pl.BlockSpec((d, GATE_LANES), lambda i, j: (0, 0))],
        out_specs=out_specs,
        out_shape=out_shape,
        scratch_shapes=[pltpu.VMEM((tm, d), BF16)],
        compiler_params=pltpu.CompilerParams(
            dimension_semantics=("parallel", "arbitrary"),
            vmem_limit_bytes=VMEM_LIMIT_BYTES),
        name="inproj",
    )(*parts, p["g_mix"], p["w_z"], p["w_v"], p["w_gates"])
    if moe_out is not None:
        return res
    return [x] + list(res)


def _route(xn2, wrh_ref, wrl_ref, brt_ref, rid_ref, rw_ref):
    t = xn2.shape[0]
    xh, xl = _split_bf16(xn2)
    logit = (_dot_nt(wrh_ref[...], xh) + _dot_nt(wrh_ref[...], xl) + _dot_nt(wrl_ref[...], xh)
             + brt_ref[...])
    ng, ne = N_GROUPS, EXP_PER_GROUP
    gl = logit[0:ng]
    gidx = lax.broadcasted_iota(jnp.int32, (ng, t), 0).astype(F32)
    idx = lax.broadcasted_iota(jnp.int32, (ne, t), 0).astype(F32)
    gmax = jnp.max(gl, axis=0, keepdims=True)
    g_idx = jnp.min(jnp.where(gl == gmax, gidx, float(ng)), axis=0, keepdims=True)
    p_g = 1.0 / jnp.sum(jnp.exp(gl - gmax), axis=0, keepdims=True)
    el = jnp.zeros((ne, t), F32)
    for g in range(ng):
        el = jnp.where(g_idx == float(g), logit[ng + g * ne:ng + (g + 1) * ne], el)
    v1 = jnp.max(el, axis=0, keepdims=True)
    i1 = jnp.min(jnp.where(el == v1, idx, float(ne)), axis=0, keepdims=True)
    rest = jnp.where(idx == i1, -jnp.inf, el)
    v2 = jnp.max(rest, axis=0, keepdims=True)
    i2 = jnp.min(jnp.where(rest == v2, idx, float(ne)), axis=0, keepdims=True)
    e21 = jnp.exp(v2 - v1)
    w1 = (1.0 / (1.0 + e21)) * p_g
    w2 = (e21 / (1.0 + e21)) * p_g
    e1 = g_idx * float(ne) + i1
    e2 = g_idx * float(ne) + i2
    slot = lax.broadcasted_iota(jnp.int32, (V7X_SUBLANES, t), 0)
    rid_ref[0] = jnp.where(slot == 0, e1, jnp.where(slot == 1, e2, 0.0)).astype(jnp.int32)
    rw_ref[0] = jnp.where(slot == 0, w1, jnp.where(slot == 1, w2, 0.0))


def _mixer_kernel(*refs, chunk, heads, nt, n_tiles):
    wrh_ref, wrl_ref, brt_ref = refs[24:27]
    rid_ref, rw_ref = refs[29:31]
    xp_s = refs[-1]
    s = pl.program_id(0)

    @pl.when(s == 0)
    def _():
        xp_s[...] = jnp.zeros(xp_s.shape, F32)

    @pl.when(s < n_tiles)
    def _():
        _mixer_tile(*refs, chunk=chunk, heads=heads, step=s % nt, last_step=nt - 1)

    @pl.when(s == n_tiles)
    def _():
        _route(xp_s[...], wrh_ref, wrl_ref, brt_ref, rid_ref, rw_ref)


def _mixer_tile(zc_ref, zo_ref, zga_ref, zgb_ref, vt_ref, zg_ref, x_ref,
                conv0_ref, h0_ref, c0_ref, n0_ref, m0_ref,
                cw_ref, cb_ref, wri_ref, br_ref, bi_ref, lam_ref, gbias_ref, mgn_ref,
                wpa_ref, wpb_ref, wout_ref, gffn_ref, wrh_ref, wrl_ref, brt_ref,
                xnew_ref, xn2_ref, rid_ref, rw_ref,
                convs_ref, hlast_ref, cout_ref, nout_ref, mout_ref,
                zbuf, xl_s, q_s, k_s, a_s, u_s, h_s, hm_s, c_s, n_s, m_s, hc_s, xp_s,
                *, chunk, heads, step, last_step):
    t = x_ref.shape[0]
    d = x_ref.shape[1]
    hd = d // heads
    tail = V7X_SUBLANES

    @pl.when(step == 0)
    def _init():
        zbuf[0:tail, :] = conv0_ref[0]
        hc_s[...] = h0_ref[0]
        c_s[...] = c0_ref[0]
        n_s[...] = n0_ref[0]
        m_s[...] = m0_ref[0]

    zbuf[tail:tail + t, :] = zc_ref[...]

    lead_rows = lax.broadcasted_iota(jnp.int32, (tail, d), 0)

    def conv_cols(lo, hi):
        cur = zbuf[tail:tail + t, lo:hi]
        prev = zbuf[0:tail, lo:hi]
        acc = None
        for tap in range(CONV_W):
            s = CONV_W - 1 - tap
            if s == 0:
                term = cur
            else:
                rolled = pltpu.roll(cur, s, 0)
                head = jnp.where(lead_rows < s, pltpu.roll(prev, s, 0), rolled[0:tail])
                term = jnp.concatenate([head, rolled[tail:]], axis=0) if t > tail else head
            term = term * cw_ref[tap:tap + 1, lo:hi]
            acc = term if acc is None else acc + term
        return acc + cb_ref[:, lo:hi]

    q_s[...] = _silu(conv_cols(d, 2 * d)).astype(BF16)
    k_s[...] = (_silu(conv_cols(2 * d, 3 * d)) * (hd ** -0.5)).astype(BF16)

    tc = chunk
    gates = zg_ref[...] + gbias_ref[...]
    lf = _log_sigmoid(gates)
    tri_r = lax.broadcasted_iota(jnp.int32, (tc, tc), 0)
    tri_c = lax.broadcasted_iota(jnp.int32, (tc, tc), 1)
    causal = tri_c <= tri_r
    tri = causal.astype(F32)
    pick = (lax.broadcasted_iota(jnp.int32, (V7X_SUBLANES, GATE_LANES), 0) ==
            lax.broadcasted_iota(jnp.int32, (V7X_SUBLANES, GATE_LANES), 1)).astype(F32)
    for j in range(t // tc):
        r0, r1 = j * tc, (j + 1) * tc
        bcum = jnp.dot(tri, lf[r0:r1], precision=lax.Precision.HIGHEST,
                       preferred_element_type=F32)
        b_al = pltpu.roll(bcum, GATE_LANES - heads, 1)
        g_al = gates[r0:r1] - b_al
        g_rows = _dot_nt(pick, g_al, precision=lax.Precision.HIGHEST)
        for h in range(heads):
            c0, c1 = h * hd, (h + 1) * hd
            qh = q_s[r0:r1, c0:c1]
            kh = k_s[r0:r1, c0:c1]
            vth = vt_ref[0, c0:c1, r0:r1].astype(BF16)
            bcol = b_al[:, h:h + 1]
            gcol = g_al[:, h:h + 1]
            m_prev = m_s[h:h + 1, 0:1]
            dlog = jnp.where(causal, bcol + g_rows[h:h + 1, :], -jnp.inf)
            m_inter = bcol + m_prev
            m_t = jnp.maximum(m_inter, jnp.max(dlog, axis=-1, keepdims=True))
            s = _dot_nt(qh, kh) * jnp.exp(dlog - m_t)
            inter = jnp.exp(m_inter - m_t)
            qc_ = _dot_nt(qh, c_s[h].astype(BF16))
            num = _dot_nt(s.astype(BF16), vth) + inter * qc_
            qn = jnp.sum(qh.astype(F32) * n_s[h:h + 1, :], axis=-1, keepdims=True)
            den = jnp.sum(s, axis=-1, keepdims=True) + inter * qn
            hm_s[r0:r1, c0:c1] = num / jnp.maximum(jnp.abs(den), jnp.exp(-m_t))
            b_end = bcol[tc - 1:tc, :]
            m_new = m_t[tc - 1:tc, :]
            wk = jnp.exp(b_end + gcol - m_new)
            decay = jnp.exp(b_end + m_prev - m_new)
            kw = wk * kh.astype(F32)
            c_s[h] = decay * c_s[h] + _dot(vth, kw.astype(BF16))
            n_s[h:h + 1, :] = decay * n_s[h:h + 1, :] + jnp.sum(kw, axis=0, keepdims=True)
            m_s[h:h + 1, :] = jnp.broadcast_to(m_new, (1, V7X_LANES))

    _route(xp_s[...], wrh_ref, wrl_ref, brt_ref, rid_ref, rw_ref)

    xl_s[...] = conv_cols(0, d)
    zbuf[0:tail, :] = zbuf[t:t + tail, :]
    gw = V7X_MXU_DIM
    sp = _softplus(-lam_ref[...])
    for g in range(d // gw):
        lo, hi = g * gw, (g + 1) * gw
        xg = xl_s[:, lo:hi]
        ri = _dot(xg.astype(BF16), wri_ref[g])
        r = _sigmoid(ri[:, :gw] + br_ref[:, lo:hi])
        ig = _sigmoid(ri[:, gw:] + bi_ref[:, lo:hi])
        log_a = (-LRU_C * r) * sp[:, lo:hi]
        a = jnp.exp(log_a)
        sq = -jnp.tanh(log_a) * (a * a + 1.0)
        mult = jnp.where(sq > 0.0, sq * lax.rsqrt(sq), 0.0)
        a_s[:, lo:hi] = a
        u_s[:, lo:hi] = mult * (ig * xg)

    rows = lax.broadcasted_iota(jnp.int32, (V7X_SUBLANES, d), 0)
    shifts = (1, 2, 4)
    masks = [rows >= s for s in shifts]
    hc = hc_s[...]
    for gi in range(t // V7X_SUBLANES):
        r0 = gi * V7X_SUBLANES
        a = a_s[r0:r0 + V7X_SUBLANES, :]
        u = u_s[r0:r0 + V7X_SUBLANES, :]
        for s, mk in zip(shifts, masks):
            a_sh = jnp.where(mk, pltpu.roll(a, s, 0), 1.0)
            u_sh = jnp.where(mk, pltpu.roll(u, s, 0), 0.0)
            u = a * u_sh + u
            a = a * a_sh
        hg = u + a * hc
        h_s[r0:r0 + V7X_SUBLANES, :] = hg
        hc = hg[V7X_SUBLANES - 1:V7X_SUBLANES, :]
    hc_s[...] = hc

    for h in range(heads):
        c0, c1 = h * hd, (h + 1) * hd
        seg = hm_s[:, c0:c1]
        seg = seg * lax.rsqrt(jnp.mean(seg * seg, axis=-1, keepdims=True) + EPS)
        hm_s[:, c0:c1] = (seg * mgn_ref[:, c0:c1]) * _sigmoid(zo_ref[:, c0:c1])
    pa = _dot(h_s[...].astype(BF16), wpa_ref[...])
    pb = _dot(hm_s[...].astype(BF16), wpb_ref[...])
    merged = _sigmoid(zga_ref[...]) * pa + _sigmoid(zgb_ref[...]) * pb
    xnew = x_ref[...] + _dot(merged.astype(BF16), wout_ref[...])
    xnew_ref[...] = xnew
    xn2 = _rmsnorm(xnew, gffn_ref[...])
    _rows_to_tiles(xn2_ref, xn2)
    xp_s[...] = xn2

    @pl.when(step == last_step)
    def _final_state():
        convs_ref[0] = zbuf[0:tail, :]
        hlast_ref[0] = hc_s[...]
        cout_ref[0] = c_s[...]
        nout_ref[0] = n_s[...]
        mout_ref[0] = m_s[...]


def _mixer(z, vt, zg, x, conv0p, h0, c0, n0, m0p, p, *, batch, seq, t, chunk):
    n, d = x.shape
    heads = n0.shape[1]
    hd = n0.shape[2]
    nt = seq // t
    cc = conv0p.shape[2]
    rt = V7X_SUBLANES
    n_tiles = batch * nt
    tile = lambda s: jnp.minimum(s, n_tiles - 1)
    zblk = lambda width, col: pl.BlockSpec((t, width), lambda s: (tile(s), col))
    const = lambda shape: pl.BlockSpec(shape, lambda s: (0,) * len(shape))
    per_b = lambda shape: pl.BlockSpec((1,) + shape, lambda s: (tile(s) // nt,) + (0,) * len(shape))
    tok = pl.BlockSpec((t, d), lambda s: (tile(s), 0))
    in_specs = [
        zblk(cc, 0), zblk(d, cc // d), zblk(d, cc // d + 1), zblk(d, cc // d + 2),
        pl.BlockSpec((1, d, t), lambda s: (tile(s), 0, 0)),
        zblk(GATE_LANES, 0), tok,
        per_b((rt, cc)), per_b((1, d)), per_b((heads, hd, hd)), per_b((heads, hd)),
        per_b((rt, V7X_LANES)),
        const((CONV_W, cc)), const((1, cc)), const(p["wri"].shape), const((1, d)), const((1, d)),
        const((1, d)), const((1, GATE_LANES)), const((1, d)),
        const((d, d)), const((d, d)), const((d, d)), const((1, d)),
        const((ROUTE_ROWS, d)), const((ROUTE_ROWS, d)), const((ROUTE_ROWS, 1)),
    ]
    route_blk = pl.BlockSpec((1, rt, t), lambda s: (jnp.maximum(s - 1, 0), 0, 0))
    out_specs = (
        tok, pl.BlockSpec((t * rt, V7X_LANES), lambda s: (tile(s), 0)), route_blk, route_blk,
        per_b((rt, cc)), per_b((1, d)), per_b((heads, hd, hd)), per_b((heads, hd)),
        per_b((rt, V7X_LANES)),
    )
    out_shape = (
        jax.ShapeDtypeStruct((n, d), F32), jax.ShapeDtypeStruct((n * rt, V7X_LANES), F32),
        jax.ShapeDtypeStruct((batch * nt, rt, t), jnp.int32),
        jax.ShapeDtypeStruct((batch * nt, rt, t), F32),
        jax.ShapeDtypeStruct((batch, rt, cc), F32),
        jax.ShapeDtypeStruct((batch, 1, d), F32),
        jax.ShapeDtypeStruct((batch, heads, hd, hd), F32),
        jax.ShapeDtypeStruct((batch, heads, hd), F32),
        jax.ShapeDtypeStruct((batch, rt, V7X_LANES), F32),
    )
    scratch = [
        pltpu.VMEM((t + rt, cc), F32),
        pltpu.VMEM((t, d), F32),
        pltpu.VMEM((t, d), BF16),
        pltpu.VMEM((t, d), BF16),
        pltpu.VMEM((t, d), F32),
        pltpu.VMEM((t, d), F32),
        pltpu.VMEM((t, d), F32),
        pltpu.VMEM((t, d), F32),
        pltpu.VMEM((heads, hd, hd), F32),
        pltpu.VMEM((heads, hd), F32),
        pltpu.VMEM((rt, V7X_LANES), F32),
        pltpu.VMEM((1, d), F32),
        pltpu.VMEM((t, d), F32),
    ]
    return pl.pallas_call(
        functools.partial(_mixer_kernel, chunk=chunk, heads=heads, nt=nt, n_tiles=n_tiles),
        grid=(n_tiles + 1,),
        in_specs=in_specs,
        out_specs=out_specs,
        out_shape=out_shape,
        scratch_shapes=scratch,
        compiler_params=pltpu.CompilerParams(
            dimension_semantics=("arbitrary",),
            vmem_limit_bytes=VMEM_LIMIT_BYTES),
        name="mixer",
    )(z, z, z, z, vt, zg, x, conv0p, h0, c0, n0, m0p,
      p["conv_w"], p["conv_b"], p["wri"], p["b_r"], p["b_i"], p["lam"], p["gate_bias"], p["m_gn"],
      p["w_pa"], p["w_pb"], p["w_out"], p["g_ffn"], p["wrt_hi"], p["wrt_lo"], p["brt"])


def _expert_kernel(te_ref, nv_ref, tok_ref, tokn_ref, dst_ref, wcol_ref, xn_hbm, wgu_ref, wd_ref, o_hbm,
                   xbuf, ybuf, sem_in, sem_out):
    del te_ref
    j = pl.program_id(0)
    nv = nv_ref[0]
    rt = V7X_SUBLANES
    tm = wcol_ref.shape[0]
    d = wgu_ref.shape[1]
    de = wd_ref.shape[1]
    slot = j % 2

    def row_in(tok, r, sl):
        return pltpu.make_async_copy(xn_hbm.at[pl.ds(pl.multiple_of(tok, rt), rt)],
                                     xbuf.at[sl, pl.ds(r * rt, rt)], sem_in.at[sl])

    def row_out(dst, r, sl):
        return pltpu.make_async_copy(ybuf.at[sl, pl.ds(r * rt, rt)],
                                     o_hbm.at[pl.ds(pl.multiple_of(dst, rt), rt)], sem_out.at[sl])

    def gather(idx_ref, sl):
        for r in range(tm):
            row_in(idx_ref[0, 0, r], r, sl).start()

    def gather_wait(sl):
        for r in range(tm):
            row_in(0, r, sl).wait()

    def scatter(sl):
        for r in range(tm):
            row_out(dst_ref[0, 0, r], r, sl).start()

    def scatter_wait(sl):
        for r in range(tm):
            row_out(0, r, sl).wait()

    @pl.when(j == 0)
    def _zero_padding_rows():
        pad0 = o_hbm.shape[0] - 2 * tm * rt
        ybuf[1] = jnp.zeros(ybuf.shape[1:], F32)
        fills = [pltpu.make_async_copy(ybuf.at[1], o_hbm.at[pl.ds(pad0 + half * tm * rt, tm * rt)],
                                       sem_out.at[1]) for half in range(2)]
        for cp in fills:
            cp.start()
        for cp in fills:
            cp.wait()

    @pl.when(j < nv)
    def _tile():
        @pl.when(j == 0)
        def _():
            gather(tok_ref, slot)

        @pl.when(j + 1 < nv)
        def _():
            gather(tokn_ref, 1 - slot)

        gather_wait(slot)

        x = _rows_from_tiles(xbuf.at[slot], tm, d).astype(BF16)
        gu = _dot(x, wgu_ref[0])
        hid = _silu(gu[:, :de]) * gu[:, de:]
        y = _dot(hid.astype(BF16), wd_ref[0]) * wcol_ref[...]
        _rows_to_tiles(ybuf.at[slot], y)

        scatter(slot)

        @pl.when(j >= 1)
        def _():
            scatter_wait(1 - slot)

        @pl.when(j == nv - 1)
        def _():
            scatter_wait(slot)


def _experts(xn2t, e_flat, w_flat, wgu, wd, *, n_tokens, tm):
    n = n_tokens
    n_exp, de, d = wd.shape
    na = e_flat.shape[0]
    rt = V7X_SUBLANES
    p_rows = na + n_exp * tm
    ntile = p_rows // tm

    cnt = jnp.sum((e_flat[:, None] == jnp.arange(n_exp, dtype=jnp.int32)[None, :]).astype(jnp.int32), axis=0)
    pad = (-cnt) % tm
    pad_key = jnp.where(lax.broadcasted_iota(jnp.int32, (n_exp, tm), 1) < pad[:, None],
                        lax.broadcasted_iota(jnp.int32, (n_exp, tm), 0), n_exp).reshape(n_exp * tm)
    keys = jnp.concatenate([e_flat, pad_key])
    wts = jnp.concatenate([w_flat, jnp.zeros((n_exp * tm,), F32)])
    shift = (p_rows - 1).bit_length()
    assert (n_exp + 1) << shift < 2**31
    pos = lax.iota(jnp.int32, p_rows)
    packed_s, w_s = lax.sort((keys * (1 << shift) + pos, wts), num_keys=1)
    key_s = packed_s >> shift
    src_s = packed_s & ((1 << shift) - 1)
    is_pad = src_s >= na
    tok_s = jnp.where(is_pad, 0, jnp.where(src_s >= n, src_s - n, src_s))
    dst_s = jnp.where(is_pad, na + ((pos // tm) % 2) * tm + pos % tm, src_s)
    tile_key = key_s[::tm]
    n_valid = jnp.sum((tile_key < n_exp).astype(jnp.int32)).reshape(1)
    tile_e = jnp.minimum(tile_key, n_exp - 1)

    idx_blk = lambda fn: pl.BlockSpec((1, 1, tm), fn, memory_space=pltpu.SMEM)
    grid_spec = pltpu.PrefetchScalarGridSpec(
        num_scalar_prefetch=2,
        grid=(ntile,),
        in_specs=[
            idx_blk(lambda j, te, nv: (j, 0, 0)),
            idx_blk(lambda j, te, nv: (jnp.minimum(j + 1, ntile - 1), 0, 0)),
            idx_blk(lambda j, te, nv: (j, 0, 0)),
            pl.BlockSpec((tm, 1), lambda j, te, nv: (j, 0)),
            pl.BlockSpec(memory_space=pl.ANY),
            pl.BlockSpec((1, d, 2 * de), lambda j, te, nv: (te[j], 0, 0)),
            pl.BlockSpec((1, de, d), lambda j, te, nv: (te[j], 0, 0)),
        ],
        out_specs=pl.BlockSpec(memory_space=pl.ANY),
        scratch_shapes=[
            pltpu.VMEM((2, tm * rt, V7X_LANES), F32), pltpu.VMEM((2, tm * rt, V7X_LANES), F32),
            pltpu.SemaphoreType.DMA((2,)), pltpu.SemaphoreType.DMA((2,)),
        ],
    )
    tok3 = (tok_s * rt).reshape(ntile, 1, tm)
    dst_s = dst_s * rt
    return pl.pallas_call(
        _expert_kernel,
        grid_spec=grid_spec,
        out_shape=jax.ShapeDtypeStruct(((na + 2 * tm) * rt, V7X_LANES), F32),
        compiler_params=pltpu.CompilerParams(
            dimension_semantics=("arbitrary",),
            vmem_limit_bytes=VMEM_LIMIT_BYTES),
        name="experts",
    )(tile_e, n_valid, tok3, tok3, dst_s.reshape(ntile, 1, tm), w_s.reshape(p_rows, 1), xn2t, wgu, wd)


def _final_kernel(x_ref, o0_ref, o1_ref, g_ref, y_ref):
    tm, d = x_ref.shape
    x = x_ref[...] + _rows_from_tiles(o0_ref, tm, d) + _rows_from_tiles(o1_ref, tm, d)
    y_ref[...] = _rmsnorm(x, g_ref[...])


def _final(x, moe_out, g, *, tm):
    n, d = x.shape
    nrt = n // tm
    rt = V7X_SUBLANES
    return pl.pallas_call(
        _final_kernel,
        grid=(nrt,),
        in_specs=[pl.BlockSpec((tm, d), lambda i: (i, 0)),
                  pl.BlockSpec((tm * rt, V7X_LANES), lambda i: (i, 0)),
                  pl.BlockSpec((tm * rt, V7X_LANES), lambda i: (i + nrt, 0)),
                  pl.BlockSpec((1, d), lambda i: (0, 0))],
        out_specs=pl.BlockSpec((tm, d), lambda i: (i, 0)),
        out_shape=jax.ShapeDtypeStruct((n, d), F32),
        compiler_params=pltpu.CompilerParams(dimension_semantics=("parallel",)),
        name="final_norm",
    )(x, moe_out, moe_out, g)


def _block_diag_tiles(w):
    nb, bw, _ = w.shape
    per = V7X_MXU_DIM // bw
    w4 = w.reshape(nb // per, per, bw, bw)
    eye = jnp.eye(per, dtype=w.dtype)
    return jnp.einsum("gicd,ij->gicjd", w4, eye).reshape(nb // per, per * bw, per * bw)


def _layer_params(l, g_mix, w_in, b_igate, b_fgate, conv_w, conv_b, lru_wr, lru_br, lru_wi, lru_bi,
                  lru_lambda, m_gn, w_pa, w_pb, w_out, g_ffn, w_group, b_group, w_erouter, b_erouter,
                  w_gate, w_up, w_down):
    d = w_in.shape[1]
    cc = conv_w.shape[2]
    heads = b_igate.shape[1]
    dm = m_gn.shape[1]
    assert dm == d and lru_lambda.shape[1] == d and cc == 3 * d
    w = w_in[l]
    v0, o0, gate0 = cc, cc + dm, cc + 2 * dm
    w_z = jnp.concatenate([w[:, :v0], w[:, o0:gate0], w[:, gate0 + 2 * heads:]], axis=1).astype(BF16)
    w_v = w[:, v0:o0].astype(BF16)
    w_gates = jnp.pad(w[:, gate0:gate0 + 2 * heads], ((0, 0), (0, GATE_LANES - 2 * heads))).astype(BF16)
    gate_bias = jnp.pad(jnp.concatenate([b_igate[l], b_fgate[l]]), (0, GATE_LANES - 2 * heads))[None]
    wri = jnp.concatenate([_block_diag_tiles(lru_wr[l]), _block_diag_tiles(lru_wi[l])], axis=2).astype(BF16)
    ng, _, ne = w_erouter.shape[1:]
    wr = jnp.concatenate([w_group[l], jnp.transpose(w_erouter[l], (1, 0, 2)).reshape(d, ng * ne)], axis=1)
    br = jnp.concatenate([b_group[l], b_erouter[l].reshape(ng * ne)])
    rpad = ROUTE_ROWS - wr.shape[1]
    wrt = jnp.pad(wr, ((0, 0), (0, rpad))).T
    wrt_hi = wrt.astype(BF16)
    wrt_lo = (wrt - wrt_hi.astype(F32)).astype(BF16)
    return dict(
        g_mix=g_mix[l][None], w_z=w_z, w_v=w_v, w_gates=w_gates, gate_bias=gate_bias,
        conv_w=conv_w[l], conv_b=conv_b[l][None], wri=wri,
        b_r=lru_br[l][None], b_i=lru_bi[l][None], lam=lru_lambda[l][None], m_gn=m_gn[l][None],
        w_pa=w_pa[l].astype(BF16), w_pb=w_pb[l].astype(BF16), w_out=w_out[l].astype(BF16),
        g_ffn=g_ffn[l][None], wrt_hi=wrt_hi, wrt_lo=wrt_lo, brt=jnp.pad(br, (0, rpad))[:, None],
        wgu=jnp.concatenate([w_gate[l], w_up[l]], axis=2).astype(BF16), wd=w_down[l].astype(BF16),
    )


def _tile_sizes(n, seq, n_exp):
    t = min(256, seq)
    chunk = min(256, t)
    tm_first = min(1024, n)
    tm = min(512, n)
    tm_exp = 512 if 2 * n >= 8 * n_exp * 512 else 128
    return t, chunk, tm_first, tm, tm_exp


def _trunk(x, conv0, h00, c00, n00, m00, layers, g_final):
    batch, seq, d = x.shape
    n = batch * seq
    depth = len(layers)
    t, chunk, tm_first, tm, tm_exp = _tile_sizes(n, seq, layers[0]["wd"].shape[0])
    xf = x.reshape(n, d)
    moe_out = None
    convs, hs, cs, ns, ms = [], [], [], [], []
    heads = n00.shape[2]
    rt = V7X_SUBLANES
    for l in range(depth):
        p = layers[l]
        if moe_out is None:
            tm_in, col_blocks = tm_first, 6
        else:
            tm_in, col_blocks = tm, 3
        xf, z, vt, zg = _inproj(xf, moe_out, p, tm=tm_in, tn=p["w_z"].shape[1] // col_blocks, t=t)
        conv0p = jnp.pad(conv0[l], ((0, 0), (rt - (CONV_W - 1), 0), (0, 0)))
        m0p = jnp.broadcast_to(
            jnp.pad(m00[l], ((0, 0), (0, rt - heads)))[:, :, None], (batch, rt, V7X_LANES))
        (xf, xn2t, rid, rw, conv_s, h_last, c_new, n_new, m_new) = _mixer(
            z, vt, zg, xf, conv0p, h00[l][:, None, :], c00[l], n00[l], m0p, p,
            batch=batch, seq=seq, t=t, chunk=chunk)
        e_flat = jnp.concatenate([rid[:, 0, :].reshape(n), rid[:, 1, :].reshape(n)])
        w_flat = jnp.concatenate([rw[:, 0, :].reshape(n), rw[:, 1, :].reshape(n)])
        moe_out = _experts(xn2t, e_flat, w_flat, p["wgu"], p["wd"], n_tokens=n, tm=tm_exp)
        convs.append(conv_s[:, rt - (CONV_W - 1):, :])
        hs.append(h_last[:, 0, :])
        cs.append(c_new)
        ns.append(n_new)
        ms.append(m_new[:, :heads, 0])
    y = _final(xf, moe_out, g_final[None], tm=tm).reshape(batch, seq, d)
    return y, jnp.stack(convs), jnp.stack(hs), jnp.stack(cs), jnp.stack(ns), jnp.stack(ms)


def kernel(x_prompt, x_sample, state_conv, state_lru, state_mlstm_C, state_mlstm_n, state_mlstm_m,
           g_mix, w_in, b_igate, b_fgate, conv_w, conv_b, lru_wr, lru_br, lru_wi, lru_bi, lru_lambda,
           m_gn, w_pa, w_pb, w_out, g_ffn, w_group, b_group, w_erouter, b_erouter,
           w_gate, w_up, w_down, g_final):
    depth = w_in.shape[0]
    layers = [
        _layer_params(l, g_mix, w_in, b_igate, b_fgate, conv_w, conv_b, lru_wr, lru_br, lru_wi, lru_bi,
                      lru_lambda, m_gn, w_pa, w_pb, w_out, g_ffn, w_group, b_group, w_erouter,
                      b_erouter, w_gate, w_up, w_down)
        for l in range(depth)]
    dt = x_prompt.dtype
    bp = x_prompt.shape[0]
    zeros_like_state = lambda s: jnp.zeros((depth, bp) + s.shape[2:], dt)
    y_s, s_conv, s_lru, s_c, s_n, s_m = _trunk(
        x_sample, state_conv, state_lru, state_mlstm_C, state_mlstm_n, state_mlstm_m,
        layers, g_final)
    y_p, p_conv, p_lru, p_c, p_n, p_m = _trunk(
        x_prompt, zeros_like_state(state_conv), zeros_like_state(state_lru),
        zeros_like_state(state_mlstm_C), zeros_like_state(state_mlstm_n),
        zeros_like_state(state_mlstm_m), layers, g_final)
    return (y_p, y_s, p_conv, p_lru, p_c, p_n, p_m, s_conv, s_lru, s_c, s_n, s_m)
```
